```python
import jax, jax.numpy as jnp
from jax import lax
import numpy as np


D_MODEL = 1024
BATCH = 16
SEQ = 4096
DEPTH = 1

MIX_WIDTH = D_MODEL
GLA_HEADS = 4
GLA_WIDTH = MIX_WIDTH // 2
GLA_DV = GLA_WIDTH // GLA_HEADS
GLA_DK = GLA_DV // 2
GLA_QK = GLA_HEADS * GLA_DK
GLA_GATE_RANK = 16
GLA_GATE_NORM = 16.0
GDN_HEADS = 4
GDN_WIDTH = MIX_WIDTH - GLA_WIDTH
GDN_DK = GDN_WIDTH // GDN_HEADS
GDN_DV = GDN_DK
CONV_WIDTH = 4
CHUNK = 64
LN_EPS = 1e-5
RMS_EPS = 1e-6
ALPHA = (2.0 * DEPTH) ** 0.25
BETA_INIT = (8.0 * DEPTH) ** -0.25

IN_SIZES = (
    GLA_QK,
    GLA_QK,
    GLA_WIDTH,
    GLA_GATE_RANK,
    GLA_WIDTH,
    3 * GDN_WIDTH,
    GDN_HEADS,
    GDN_HEADS,
    GDN_WIDTH,
)
IN_COLS = sum(IN_SIZES)

kernel_name = 'hymba_gla_gdn_deepnorm_adaln'


def _split_cols(t, sizes):
    idx = []
    acc = 0
    for s in sizes[:-1]:
        acc += s
        idx.append(acc)
    return jnp.split(t, idx, axis=-1)


def layer_norm(u, w, b):
    u32 = u.astype(jnp.float32)
    mu = jnp.mean(u32, axis=-1, keepdims=True)
    var = jnp.mean(jnp.square(u32 - mu), axis=-1, keepdims=True)
    return ((u32 - mu) * lax.rsqrt(var + LN_EPS) * w + b).astype(u.dtype)


def rms_norm(u, w):
    u32 = u.astype(jnp.float32)
    return (u32 * lax.rsqrt(jnp.mean(jnp.square(u32), axis=-1, keepdims=True) + RMS_EPS) * w).astype(u.dtype)


def l2_norm(u):
    u32 = u.astype(jnp.float32)
    return (u32 * lax.rsqrt(jnp.sum(jnp.square(u32), axis=-1, keepdims=True) + RMS_EPS)).astype(u.dtype)


def causal_depthwise_conv(u, w):
    K, C = w.shape
    return lax.conv_general_dilated(u, w[:, None, :].astype(u.dtype), window_strides=(1,),
                                    padding=((K - 1, 0),), dimension_numbers=('NWC', 'WIO', 'NWC'),
                                    feature_group_count=C)


def _to_chunks(t, n_chunks):
    B, T, H = t.shape[:3]
    t = t.reshape((B, n_chunks, CHUNK, H) + t.shape[3:])
    return jnp.moveaxis(t, 3, 1)


def _from_chunks(t):
    B, H, N, C, d = t.shape
    return jnp.moveaxis(t, 1, 3).reshape(B, N * C, H, d)


def gla_chunked(q, k, v, g):
    out_dtype = v.dtype
    B, T, H, dk = q.shape
    dv = v.shape[-1]
    N = T // CHUNK
    q, k, v, g = (_to_chunks(t.astype(jnp.float32), N) for t in (q, k, v, g))
    b = jnp.cumsum(g, axis=3)
    b_ref = b[:, :, :, CHUNK // 2 - 1:CHUNK // 2, :]
    causal = jnp.tril(jnp.ones((CHUNK, CHUNK), dtype=bool))
    att = jnp.einsum('bhnid,bhnjd->bhnij', q * jnp.exp(b - b_ref), k * jnp.exp(b_ref - b))
    att = jnp.where(causal, att, 0.0)
    o_intra = jnp.einsum('bhnij,bhnjv->bhniv', att, v)
    b_last = b[:, :, :, -1, :]
    upd = jnp.einsum('bhncd,bhncv->bhndv', k * jnp.exp(b_last[:, :, :, None, :] - b), v)
    decay = jnp.exp(b_last)

    def step(S, inp):
        d_n, u_n = inp
        return d_n[..., None] * S + u_n, S

    S0 = jnp.zeros((B, H, dk, dv), jnp.float32)
    _, S_prev = lax.scan(step, S0, (jnp.moveaxis(decay, 2, 0), jnp.moveaxis(upd, 2, 0)))
    S_prev = jnp.moveaxis(S_prev, 0, 2)
    o_inter = jnp.einsum('bhncd,bhndv->bhncv', q * jnp.exp(b), S_prev)
    return _from_chunks(o_intra + o_inter).astype(out_dtype)


def gated_delta_chunked(q, k, v, g, beta):
    out_dtype = v.dtype
    B, T, H, dk = q.shape
    dv = v.shape[-1]
    N = T // CHUNK
    q, k, v = (_to_chunks(t.astype(jnp.float32), N) for t in (q, k, v))
    g, beta = (_to_chunks(t.astype(jnp.float32), N) for t in (g, beta))
    d = jnp.cumsum(g, axis=-1)
    causal = jnp.tril(jnp.ones((CHUNK, CHUNK), dtype=bool))
    strict = jnp.tril(jnp.ones((CHUNK, CHUNK), dtype=bool), k=-1)
    L = jnp.exp(jnp.where(causal, d[..., :, None] - d[..., None, :], -jnp.inf))
    k_beta = k * beta[..., None]
    A = jnp.where(strict, jnp.einsum('bhnid,bhnjd->bhnij', k_beta, k) * L, 0.0)
    eye = jnp.eye(CHUNK, dtype=jnp.float32)
    rhs = jnp.concatenate([v * beta[..., None], k_beta * jnp.exp(d)[..., None]], axis=-1)
    sol = lax.linalg.triangular_solve(A + eye, rhs, left_side=True, lower=True, unit_diagonal=True)
    u, w = sol[..., :dv], sol[..., dv:]
    qk = jnp.where(causal, jnp.einsum('bhnid,bhnjd->bhnij', q, k) * L, 0.0)
    q_dec = q * jnp.exp(d)[..., None]
    k_dec = k * jnp.exp(d[..., -1:] - d)[..., None]
    chunk_decay = jnp.exp(d[..., -1])

    def step(S, inp):
        qk_n, q_n, k_n, u_n, w_n, dec_n = inp
        v_new = u_n - jnp.einsum('bhcd,bhdv->bhcv', w_n, S)
        o_n = jnp.einsum('bhcd,bhdv->bhcv', q_n, S) + jnp.einsum('bhij,bhjv->bhiv', qk_n, v_new)
        S = dec_n[..., None, None] * S + jnp.einsum('bhcd,bhcv->bhdv', k_n, v_new)
        return S, o_n

    xs = tuple(jnp.moveaxis(t, 2, 0) for t in (qk, q_dec, k_dec, u, w, chunk_decay))
    S0 = jnp.zeros((B, H, dk, dv), jnp.float32)
    _, o = lax.scan(step, S0, xs)
    return _from_chunks(jnp.moveaxis(o, 0, 2)).astype(out_dtype)


def hybrid_layer(x, c, w_ada, b_ada, w_in, gla_w_gate_up, gla_b_gate, gla_norm_w,
                 gdn_conv_w, gdn_a_log, gdn_dt_bias, gdn_norm_w, w_out, ln_w, ln_b):
    B, T, _ = x.shape
    mod = (c @ w_ada + b_ada)[:, None, :]
    shift, scale, gate = jnp.split(mod, 3, axis=-1)
    h = x * (1.0 + scale) + shift
    proj = h @ w_in
    (gla_q, gla_k, gla_v, gla_lr, gla_og, gdn_qkv, gdn_a, gdn_b, gdn_og) = _split_cols(proj, IN_SIZES)

    q = gla_q.reshape(B, T, GLA_HEADS, GLA_DK) * (GLA_DK ** -0.5)
    k = gla_k.reshape(B, T, GLA_HEADS, GLA_DK)
    v = gla_v.reshape(B, T, GLA_HEADS, GLA_DV)
    z = (gla_lr @ gla_w_gate_up + gla_b_gate).astype(jnp.float32)
    g = (jax.nn.log_sigmoid(z) / GLA_GATE_NORM).reshape(B, T, GLA_HEADS, GLA_DK)
    o_a = rms_norm(gla_chunked(q, k, v, g), gla_norm_w)
    y_a = o_a.reshape(B, T, GLA_WIDTH) * jax.nn.silu(gla_og)

    qkv = jax.nn.silu(causal_depthwise_conv(gdn_qkv, gdn_conv_w))
    q, k, v = jnp.split(qkv, 3, axis=-1)
    q = l2_norm(q.reshape(B, T, GDN_HEADS, GDN_DK)) * (GDN_DK ** -0.5)
    k = l2_norm(k.reshape(B, T, GDN_HEADS, GDN_DK))
    v = v.reshape(B, T, GDN_HEADS, GDN_DV)
    g = -jnp.exp(gdn_a_log.astype(jnp.float32)) * jax.nn.softplus((gdn_a + gdn_dt_bias).astype(jnp.float32))
    beta = jax.nn.sigmoid(gdn_b.astype(jnp.float32))
    o_b = rms_norm(gated_delta_chunked(q, k, v, g, beta), gdn_norm_w)
    y_b = o_b.reshape(B, T, GDN_WIDTH) * jax.nn.silu(gdn_og)

    y = jnp.concatenate([y_a, y_b], axis=-1) @ w_out
    return layer_norm(ALPHA * x + (1.0 + gate) * y, ln_w, ln_b)


def setup_inputs(seed: int = 0) -> dict:
    key = jax.random.key(seed)
    ks = jax.random.split(key, 16)
    f32 = jnp.float32
    x = jax.random.normal(ks[0], (BATCH, SEQ, D_MODEL), f32)
    c = jax.random.normal(ks[1], (BATCH, D_MODEL), f32)
    w_ada = jax.random.normal(ks[2], (DEPTH, D_MODEL, 3 * D_MODEL), f32) * (0.1 * D_MODEL ** -0.5)
    b_ada = jax.random.normal(ks[3], (DEPTH, 3 * D_MODEL), f32) * 0.01
    col_scale = jnp.concatenate([
        jnp.ones((2 * GLA_QK,), f32),
        jnp.full((GLA_WIDTH,), BETA_INIT, f32),
        jnp.ones((GLA_GATE_RANK + GLA_WIDTH + 2 * GDN_WIDTH,), f32),
        jnp.full((GDN_WIDTH,), BETA_INIT, f32),
        jnp.ones((2 * GDN_HEADS + GDN_WIDTH,), f32),
    ])
    w_in = jax.random.normal(ks[4], (DEPTH, D_MODEL, IN_COLS), f32) * (D_MODEL ** -0.5) * col_scale
    gla_w_gate_up = jax.random.normal(ks[5], (DEPTH, GLA_GATE_RANK, GLA_QK), f32) * (GLA_GATE_RANK ** -0.5)
    gla_b_gate = jax.random.normal(ks[6], (DEPTH, GLA_QK), f32) * 0.1
    gla_norm_w = 1.0 + 0.01 * jax.random.normal(ks[7], (DEPTH, GLA_DV), f32)
    gdn_conv_w = jax.random.normal(ks[8], (DEPTH, CONV_WIDTH, 3 * GDN_WIDTH), f32) * (CONV_WIDTH ** -0.5)
    gdn_a_log = jnp.log(jax.random.uniform(ks[9], (DEPTH, GDN_HEADS), f32, 1.0, 16.0))
    dt = jnp.exp(jax.random.uniform(ks[10], (DEPTH, GDN_HEADS), f32, np.log(1e-3), np.log(1e-1)))
    gdn_dt_bias = dt + jnp.log(-jnp.expm1(-dt))
    gdn_norm_w = 1.0 + 0.01 * jax.random.normal(ks[11], (DEPTH, GDN_DV), f32)
    w_out = jax.random.normal(ks[12], (DEPTH, MIX_WIDTH, D_MODEL), f32) * (MIX_WIDTH ** -0.5) * BETA_INIT
    ln_w = 1.0 + 0.01 * jax.random.normal(ks[13], (DEPTH, D_MODEL), f32)
    ln_b = 0.01 * jax.random.normal(ks[14], (DEPTH, D_MODEL), f32)
    return {'x': x, 'c': c, 'w_ada': w_ada, 'b_ada': b_ada, 'w_in': w_in,
            'gla_w_gate_up': gla_w_gate_up, 'gla_b_gate': gla_b_gate, 'gla_norm_w': gla_norm_w,
            'gdn_conv_w': gdn_conv_w, 'gdn_a_log': gdn_a_log, 'gdn_dt_bias': gdn_dt_bias,
            'gdn_norm_w': gdn_norm_w, 'w_out': w_out, 'ln_w': ln_w, 'ln_b': ln_b}


def reference(x, c, w_ada, b_ada, w_in, gla_w_gate_up, gla_b_gate, gla_norm_w,
              gdn_conv_w, gdn_a_log, gdn_dt_bias, gdn_norm_w, w_out, ln_w, ln_b):
    for layer in range(DEPTH):
        x = hybrid_layer(x, c, w_ada[layer], b_ada[layer], w_in[layer], gla_w_gate_up[layer],
                         gla_b_gate[layer], gla_norm_w[layer], gdn_conv_w[layer], gdn_a_log[layer],
                         gdn_dt_bias[layer], gdn_norm_w[layer], w_out[layer], ln_w[layer], ln_b[layer])
    return x
```

```python
import functools

import jax
import jax.numpy as jnp
from jax import lax
from jax.experimental import pallas as pl
from jax.experimental.pallas import tpu as pltpu

D_MODEL = 1024
GLA_HEADS = 4
GLA_WIDTH = 512
GLA_DV = 128
GLA_DK = 64
GLA_QK = GLA_HEADS * GLA_DK
GLA_GATE_RANK = 16
GLA_GATE_NORM = 16.0
GDN_HEADS = 4
GDN_WIDTH = 512
GDN_DK = 128
GDN_DV = 128
CONV_WIDTH = 4
CHUNK = 64
LN_EPS = 1e-5
RMS_EPS = 1e-6

SUBLANES = 8
LANES = 128

C_GLA_Q = 0
C_GLA_K = C_GLA_Q + GLA_QK
C_GLA_V = C_GLA_K + GLA_QK
C_GLA_OG = C_GLA_V + GLA_WIDTH
C_GDN_QKV = C_GLA_OG + GLA_WIDTH
C_GDN_OG = C_GDN_QKV + 3 * GDN_WIDTH
C_SMALL = C_GDN_OG + GDN_WIDTH
N_COLS = C_SMALL + LANES
L_LR = 0
L_A = GLA_GATE_RANK
L_B = L_A + GDN_HEADS

TIME_BLOCK = 512
VMEM_LIMIT_BYTES = 56 * 1024 * 1024

_F32 = jnp.float32
_BF16 = jnp.bfloat16


def _dot(a, b):
    return jnp.dot(a.astype(_BF16), b.astype(_BF16), preferred_element_type=_F32)


def _dot_nt(a, b):
    return lax.dot_general(a.astype(_BF16), b.astype(_BF16), (((1,), (1,)), ((), ())),
                           preferred_element_type=_F32)


def _dot_tn(a, b):
    return lax.dot_general(a.astype(_BF16), b.astype(_BF16), (((0,), (0,)), ((), ())),
                           preferred_element_type=_F32)


def _split_hi_lo(a):
    hi = a.astype(_BF16)
    lo = (a - hi.astype(_F32)).astype(_BF16)
    return hi, lo


def _sigmoid(a):
    return 1.0 / (1.0 + jnp.exp(-a))


def _softplus(a):
    return jnp.maximum(a, 0.0) + jnp.log1p(jnp.exp(-jnp.abs(a)))


def _mod_kernel(c_ref, w_ref, b_ref, o_ref):
    o_ref[...] = jnp.dot(c_ref[...], w_ref[...], preferred_element_type=_F32,
                         precision=lax.Precision.HIGHEST) + b_ref[...]


def _layer_kernel(x_ref, mod_ref, w_all_ref, wg_ref, bg_ref, normw_ref, convw_ref, alog_ref, dtb_ref,
                  w_out_ref, lnw_ref, lnb_ref, out_ref, proj_ref, y_ref, sgla_ref, sgdn_ref, *, tb, alpha):
    t = pl.program_id(1)

    @pl.when(t == 0)
    def _():
        proj_ref[0:SUBLANES, :] = jnp.zeros((SUBLANES, N_COLS), _F32)
        sgla_ref[...] = jnp.zeros_like(sgla_ref)
        sgdn_ref[...] = jnp.zeros_like(sgdn_ref)

    shift = mod_ref[0:1, :]
    scale = mod_ref[1:2, :]
    gate = mod_ref[2:3, :]
    h = x_ref[...] * (1.0 + scale) + shift
    proj_ref[SUBLANES:, :] = _dot(h, w_all_ref[...])

    row = lax.broadcasted_iota(jnp.int32, (CHUNK, CHUNK), 0)
    col = lax.broadcasted_iota(jnp.int32, (CHUNK, CHUNK), 1)
    causal = row >= col
    strict = row > col
    tri_lower = causal.astype(_BF16)
    tri_upper = (row <= col).astype(_BF16)
    neg_exp_alog = -jnp.exp(alog_ref[...])
    dt_bias = dtb_ref[...]
    conv_w = convw_ref[...]
    bg = bg_ref[...]
    norm_w = normw_ref[...]

    def chunk(c, carry):
        r0 = pl.multiple_of(c * CHUNK, CHUNK)
        rows = pl.ds(r0 + SUBLANES, CHUNK)

        small = proj_ref[rows, C_SMALL:C_SMALL + LANES]
        z = _dot(small, wg_ref[...]) + bg
        g_gla = (jnp.minimum(z, 0.0) - jnp.log1p(jnp.exp(-jnp.abs(z)))) * (1.0 / GLA_GATE_NORM)
        g_gdn = neg_exp_alog * _softplus(small + dt_bias)
        beta_all = _sigmoid(small)

        g_all = jnp.concatenate([g_gla, g_gdn], axis=1)
        g_hi, g_lo = _split_hi_lo(g_all)
        cum = (jnp.dot(tri_lower, g_hi, preferred_element_type=_F32)
               + jnp.dot(tri_lower, g_lo, preferred_element_type=_F32))
        b = cum[:, :GLA_QK]
        d_cols = cum[:, GLA_QK:]
        gd_hi, gd_lo = _split_hi_lo(g_gdn)
        d_rows = (lax.dot_general(gd_hi, tri_upper, (((0,), (0,)), ((), ())), preferred_element_type=_F32)
                  + lax.dot_general(gd_lo, tri_upper, (((0,), (0,)), ((), ())), preferred_element_type=_F32))

        q = proj_ref[rows, C_GLA_Q:C_GLA_Q + GLA_QK] * (GLA_DK ** -0.5)
        k = proj_ref[rows, C_GLA_K:C_GLA_K + GLA_QK]
        v = proj_ref[rows, C_GLA_V:C_GLA_V + GLA_WIDTH]
        b_mid = b[CHUNK // 2 - 1:CHUNK // 2, :]
        b_last = b[CHUNK - 1:CHUNK, :]
        q_intra = q * jnp.exp(b - b_mid)
        k_intra = k * jnp.exp(b_mid - b)
        q_dec = q * jnp.exp(b)
        k_dec = k * jnp.exp(b_last - b)
        state_t = sgla_ref[...]
        o_heads = []
        upd = []
        for hd in range(GLA_HEADS):
            ks = slice(hd * GLA_DK, (hd + 1) * GLA_DK)
            vh = v[:, hd * GLA_DV:(hd + 1) * GLA_DV]
            att = jnp.where(causal, _dot_nt(q_intra[:, ks], k_intra[:, ks]), 0.0)
            o_heads.append(_dot(att, vh) + _dot_nt(q_dec[:, ks], state_t[:, ks]))
            upd.append(_dot_tn(vh, k_dec[:, ks]))
        sgla_ref[...] = state_t * jnp.exp(b_last) + jnp.concatenate(upd, axis=1)

        pre = proj_ref[pl.ds(r0, CHUNK + SUBLANES), C_GDN_QKV:C_GDN_QKV + 3 * GDN_WIDTH]
        conv = conv_w[0:1, :] * pre[SUBLANES - 3:SUBLANES - 3 + CHUNK, :]
        for kk in range(1, CONV_WIDTH):
            off = SUBLANES - (CONV_WIDTH - 1) + kk
            conv = conv + conv_w[kk:kk + 1, :] * pre[off:off + CHUNK, :]
        qkv = conv * _sigmoid(conv)
        for hd in range(GDN_HEADS):
            qh = qkv[:, hd * GDN_DK:(hd + 1) * GDN_DK]
            kh = qkv[:, GDN_WIDTH + hd * GDN_DK:GDN_WIDTH + (hd + 1) * GDN_DK]
            vh = qkv[:, 2 * GDN_WIDTH + hd * GDN_DV:2 * GDN_WIDTH + (hd + 1) * GDN_DV]
            qh = qh * lax.rsqrt(jnp.sum(qh * qh, axis=-1, keepdims=True) + RMS_EPS) * (GDN_DK ** -0.5)
            kh = kh * lax.rsqrt(jnp.sum(kh * kh, axis=-1, keepdims=True) + RMS_EPS)
            beta = beta_all[:, L_B + hd:L_B + hd + 1]
            d_col = d_cols[:, L_A + hd:L_A + hd + 1]
            d_row = d_rows[L_A + hd:L_A + hd + 1, :]
            d_last = d_cols[CHUNK - 1:CHUNK, L_A + hd:L_A + hd + 1]
            decay = jnp.exp(jnp.where(causal, d_col - d_row, -jnp.inf))
            k_beta = kh * beta
            aq = _dot_nt(jnp.concatenate([k_beta, qh], axis=0), kh)
            a = jnp.where(strict, aq[:CHUNK] * decay, 0.0)
            qk = jnp.where(causal, aq[CHUNK:] * decay, 0.0)
            e_d = jnp.exp(d_col)
            sol = jnp.concatenate([vh * beta, k_beta * e_d], axis=1)
            sol = sol - _dot(a, sol)
            a_pow = a
            for _ in range(5):
                a_pow = _dot(a_pow, a_pow)
                sol = sol + _dot(a_pow, sol)
            u = sol[:, :GDN_DV]
            w = sol[:, GDN_DV:]
            q_d = qh * e_d
            k_d = kh * jnp.exp(d_last - d_col)
            state = sgdn_ref[hd]
            ws_qs = _dot(jnp.concatenate([w, q_d], axis=0), state)
            v_new = u - ws_qs[:CHUNK]
            o_heads.append(ws_qs[CHUNK:] + _dot(qk, v_new))
            sgdn_ref[hd] = jnp.exp(d_last) * state + _dot_tn(k_d, v_new)

        for hd in range(GLA_HEADS + GDN_HEADS):
            o_h = o_heads[hd]
            if hd < GLA_HEADS:
                c_og = C_GLA_OG + hd * GLA_DV
            else:
                c_og = C_GDN_OG + (hd - GLA_HEADS) * GDN_DV
            og = proj_ref[rows, c_og:c_og + LANES]
            nw = norm_w[:, hd * LANES:(hd + 1) * LANES]
            o_n = o_h * lax.rsqrt(jnp.mean(o_h * o_h, axis=-1, keepdims=True) + RMS_EPS) * nw
            y_ref[pl.ds(r0, CHUNK), hd * LANES:(hd + 1) * LANES] = (o_n * (og * _sigmoid(og))).astype(_BF16)
        return carry

    lax.fori_loop(0, tb // CHUNK, chunk, 0)

    proj_ref[0:SUBLANES, :] = proj_ref[tb:tb + SUBLANES, :]

    y = jnp.dot(y_ref[...], w_out_ref[...], preferred_element_type=_F32)
    r = alpha * x_ref[...] + (1.0 + gate) * y
    mu = jnp.mean(r, axis=-1, keepdims=True)
    rc = r - mu
    var = jnp.mean(rc * rc, axis=-1, keepdims=True)
    out_ref[...] = rc * lax.rsqrt(var + LN_EPS) * lnw_ref[...] + lnb_ref[...]


def _modulation(c, w_ada, b_ada):
    bsz, d = c.shape
    n = w_ada.shape[1]
    return pl.pallas_call(
        _mod_kernel,
        grid=(n // d,),
        in_specs=[pl.BlockSpec((bsz, d), lambda j: (0, 0)),
                  pl.BlockSpec((d, d), lambda j: (0, j)),
                  pl.BlockSpec((1, d), lambda j: (0, j))],
        out_specs=pl.BlockSpec((bsz, d), lambda j: (0, j)),
        out_shape=jax.ShapeDtypeStruct((bsz, n), _F32),
        name="adaln_modulation",
    )(c, w_ada, b_ada.reshape(1, n))


def _pack_weights(w_in, gla_w_gate_up, gdn_a_log, gdn_dt_bias, gla_norm_w, gdn_norm_w):
    o = 0
    seg = {}
    for name, size in (("gla_q", GLA_QK), ("gla_k", GLA_QK), ("gla_v", GLA_WIDTH), ("gla_lr", GLA_GATE_RANK),
                       ("gla_og", GLA_WIDTH), ("gdn_qkv", 3 * GDN_WIDTH), ("gdn_a", GDN_HEADS),
                       ("gdn_b", GDN_HEADS), ("gdn_og", GDN_WIDTH)):
        seg[name] = w_in[:, o:o + size]
        o += size
    pad = jnp.zeros((w_in.shape[0], LANES - L_B - GDN_HEADS), w_in.dtype)
    w_all = jnp.concatenate([seg["gla_q"], seg["gla_k"], seg["gla_v"], seg["gla_og"], seg["gdn_qkv"],
                             seg["gdn_og"], seg["gla_lr"], seg["gdn_a"], seg["gdn_b"], pad], axis=1)
    wg = jnp.zeros((LANES, GLA_QK), _F32).at[L_LR:L_LR + GLA_GATE_RANK, :].set(gla_w_gate_up)
    alog_row = jnp.zeros((1, LANES), _F32).at[0, L_A:L_A + GDN_HEADS].set(gdn_a_log)
    dtb_row = jnp.zeros((1, LANES), _F32).at[0, L_A:L_A + GDN_HEADS].set(gdn_dt_bias)
    norm_w = jnp.concatenate([jnp.tile(gla_norm_w, GLA_HEADS), jnp.tile(gdn_norm_w, GDN_HEADS)]).reshape(1, -1)
    return w_all.astype(_BF16), wg.astype(_BF16), alog_row, dtb_row, norm_w


def _hybrid_layer(x, c, w_ada, b_ada, w_in, gla_w_gate_up, gla_b_gate, gla_norm_w, gdn_conv_w, gdn_a_log,
                  gdn_dt_bias, gdn_norm_w, w_out, ln_w, ln_b, *, alpha):
    bsz, seq, d = x.shape
    tb = min(TIME_BLOCK, seq)
    assert d == D_MODEL and seq % tb == 0 and tb % CHUNK == 0
    mod = _modulation(c, w_ada, b_ada).reshape(bsz, 3, d)
    w_all, wg, alog_row, dtb_row, norm_w = _pack_weights(w_in, gla_w_gate_up, gdn_a_log, gdn_dt_bias,
                                                         gla_norm_w, gdn_norm_w)
    const = lambda b, t: (0, 0)
    return pl.pallas_call(
        functools.partial(_layer_kernel, tb=tb, alpha=alpha),
        grid=(bsz, seq // tb),
        in_specs=[
            pl.BlockSpec((None, tb, d), lambda b, t: (b, t, 0)),
            pl.BlockSpec((None, 3, d), lambda b, t: (b, 0, 0)),
            pl.BlockSpec((d, N_COLS), const),
            pl.BlockSpec((LANES, GLA_QK), const),
            pl.BlockSpec((1, GLA_QK), const),
            pl.BlockSpec((1, d), const),
            pl.BlockSpec((CONV_WIDTH, 3 * GDN_WIDTH), const),
            pl.BlockSpec((1, LANES), const),
            pl.BlockSpec((1, LANES), const),
            pl.BlockSpec((d, d), const),
            pl.BlockSpec((1, d), const),
            pl.BlockSpec((1, d), const),
        ],
        out_specs=pl.BlockSpec((None, tb, d), lambda b, t: (b, t, 0)),
        out_shape=jax.ShapeDtypeStruct((bsz, seq, d), _F32),
        scratch_shapes=[
            pltpu.VMEM((tb + SUBLANES, N_COLS), _F32),
            pltpu.VMEM((tb, d), _BF16),
            pltpu.VMEM((GLA_DV, GLA_QK), _F32),
            pltpu.VMEM((GDN_HEADS, GDN_DK, GDN_DV), _F32),
        ],
        compiler_params=pltpu.CompilerParams(dimension_semantics=("arbitrary", "arbitrary"),
                                             vmem_limit_bytes=VMEM_LIMIT_BYTES),
        name="hybrid_layer",
    )(x, mod, w_all, wg, gla_b_gate.reshape(1, -1), norm_w, gdn_conv_w, alog_row, dtb_row,
      w_out.astype(_BF16), ln_w.reshape(1, -1), ln_b.reshape(1, -1))


def kernel(x, c, w_ada, b_ada, w_in, gla_w_gate_up, gla_b_gate, gla_norm_w, gdn_conv_w, gdn_a_log, gdn_dt_bias, gdn_norm_w, w_out, ln_w, ln_b):
    depth = w_ada.shape[0]
    alpha = (2.0 * depth) ** 0.25
    for layer in range(depth):
        x = _hybrid_layer(x, c, w_ada[layer], b_ada[layer], w_in[layer], gla_w_gate_up[layer],
                          gla_b_gate[layer], gla_norm_w[layer], gdn_conv_w[layer], gdn_a_log[layer],
                          gdn_dt_bias[layer], gdn_norm_w[layer], w_out[layer], ln_w[layer], ln_b[layer],
                          alpha=alpha)
    return x
```

```python
import functools

import jax
import jax.numpy as jnp
from jax import lax
from jax.experimental import pallas as pl
from jax.experimental.pallas import tpu as pltpu

D_MODEL = 1024
GLA_HEADS = 4
GLA_WIDTH = 512
GLA_DV = 128
GLA_DK = 64
GLA_QK = GLA_HEADS * GLA_DK
GLA_GATE_RANK = 16
GLA_GATE_NORM = 16.0
GDN_HEADS = 4
GDN_WIDTH = 512
GDN_DK = 128
GDN_DV = 128
CONV_WIDTH = 4
CHUNK = 64
LN_EPS = 1e-5
RMS_EPS = 1e-6

SUBLANES = 8
LANES = 128

C_GLA_Q = 0
C_GLA_K = C_GLA_Q + GLA_QK
C_GLA_V = C_GLA_K + GLA_QK
C_GLA_OG = C_GLA_V + GLA_WIDTH
C_GDN_QKV = C_GLA_OG + GLA_WIDTH
C_GDN_OG = C_GDN_QKV + 3 * GDN_WIDTH
C_SMALL = C_GDN_OG + GDN_WIDTH
N_COLS = C_SMALL + LANES
L_LR = 0
L_A = GLA_GATE_RANK
L_B = L_A + GDN_HEADS

TIME_BLOCK = 512
STACK = GLA_HEADS * CHUNK
SOLVE_GROUP = 4
VMEM_LIMIT_BYTES = 56 * 1024 * 1024

_F32 = jnp.float32
_BF16 = jnp.bfloat16


def _dot(a, b):
    return jnp.dot(a.astype(_BF16), b.astype(_BF16), preferred_element_type=_F32)


def _dot_nt(a, b):
    return lax.dot_general(a.astype(_BF16), b.astype(_BF16), (((1,), (1,)), ((), ())),
                           preferred_element_type=_F32)


def _dot_tn(a, b):
    return lax.dot_general(a.astype(_BF16), b.astype(_BF16), (((0,), (0,)), ((), ())),
                           preferred_element_type=_F32)


def _split_hi_lo(a):
    hi = a.astype(_BF16)
    lo = (a - hi.astype(_F32)).astype(_BF16)
    return hi, lo


def _sigmoid(a):
    return 1.0 / (1.0 + jnp.exp(-a))


def _softplus(a):
    return jnp.maximum(a, 0.0) + jnp.log1p(jnp.exp(-jnp.abs(a)))


def _mod_kernel(c_ref, w_ref, b_ref, o_ref):
    o_ref[...] = jnp.dot(c_ref[...], w_ref[...], preferred_element_type=_F32,
                         precision=lax.Precision.HIGHEST) + b_ref[...]


def _tile_heads(a):
    return jnp.concatenate([a] * GLA_HEADS, axis=0)


def _stack_heads(a, width):
    return jnp.concatenate([a[:, hd * width:(hd + 1) * width] for hd in range(GLA_HEADS)], axis=0)


def _stack_cols(a, lane0):
    return jnp.concatenate([a[:, lane0 + hd:lane0 + hd + 1] for hd in range(GDN_HEADS)], axis=0)


def _layer_kernel(x_ref, mod_ref, w_all_ref, wg_ref, bg_ref, glanw_ref, gdnnw_ref, convw_ref, alog_ref, dtb_ref,
                  w_out_ref, lnw_ref, lnb_ref, out_ref, proj_ref, y_ref, sgla_ref, sgdn_ref, a_ref, qk_ref,
                  sol_ref, qd_ref, kd_ref, dl_ref, *, tb, alpha):
    t = pl.program_id(1)
    n_chunks = tb // CHUNK

    @pl.when(t == 0)
    def _():
        proj_ref[0:SUBLANES, :] = jnp.zeros((SUBLANES, N_COLS), _F32)
        sgla_ref[...] = jnp.zeros_like(sgla_ref)
        sgdn_ref[...] = jnp.zeros_like(sgdn_ref)

    shift = mod_ref[0:1, :]
    scale = mod_ref[1:2, :]
    gate = mod_ref[2:3, :]
    h = x_ref[...] * (1.0 + scale) + shift
    proj_ref[SUBLANES:, :] = _dot(h, w_all_ref[...])

    row = lax.broadcasted_iota(jnp.int32, (STACK, STACK), 0)
    col = lax.broadcasted_iota(jnp.int32, (STACK, STACK), 1)
    same_head = (row // CHUNK) == (col // CHUNK)
    causal = same_head & (row >= col)
    strict = same_head & (row > col)
    row_c = lax.broadcasted_iota(jnp.int32, (CHUNK, CHUNK), 0)
    col_c = lax.broadcasted_iota(jnp.int32, (CHUNK, CHUNK), 1)
    tri_lower = (row_c >= col_c).astype(_BF16)
    tri_upper4 = jnp.concatenate([(row_c <= col_c).astype(_BF16)] * GDN_HEADS, axis=1)
    lane_head = lax.broadcasted_iota(jnp.int32, (1, STACK), 1) // CHUNK
    neg_exp_alog = -jnp.exp(alog_ref[...])
    dt_bias = dtb_ref[...]
    conv_w = convw_ref[...]
    bg = bg_ref[...]

    def gated_out(o_s, og_s, nw):
        o_n = o_s * lax.rsqrt(jnp.mean(o_s * o_s, axis=-1, keepdims=True) + RMS_EPS) * nw
        return (o_n * (og_s * _sigmoid(og_s))).astype(_BF16)

    def prepare(c, carry):
        r0 = pl.multiple_of(c * CHUNK, CHUNK)
        rows = pl.ds(r0 + SUBLANES, CHUNK)

        small = proj_ref[rows, C_SMALL:C_SMALL + LANES]
        z = _dot(small, wg_ref[...]) + bg
        g_gla = (jnp.minimum(z, 0.0) - jnp.log1p(jnp.exp(-jnp.abs(z)))) * (1.0 / GLA_GATE_NORM)
        g_gdn = neg_exp_alog * _softplus(small + dt_bias)
        beta_all = _sigmoid(small)

        g_all = jnp.concatenate([g_gla, g_gdn], axis=1)
        g_hi, g_lo = _split_hi_lo(g_all)
        cum = (jnp.dot(tri_lower, g_hi, preferred_element_type=_F32)
               + jnp.dot(tri_lower, g_lo, preferred_element_type=_F32))
        b = cum[:, :GLA_QK]
        d_all = cum[:, GLA_QK:]
        gd_hi, gd_lo = _split_hi_lo(g_gdn)
        d_rows = _dot_tn(gd_hi, tri_upper4) + _dot_tn(gd_lo, tri_upper4)

        q = proj_ref[rows, C_GLA_Q:C_GLA_Q + GLA_QK] * (GLA_DK ** -0.5)
        k = proj_ref[rows, C_GLA_K:C_GLA_K + GLA_QK]
        v_s = _stack_heads(proj_ref[rows, C_GLA_V:C_GLA_V + GLA_WIDTH], GLA_DV)
        b_mid = b[CHUNK // 2 - 1:CHUNK // 2, :]
        b_last = b[CHUNK - 1:CHUNK, :]
        q_intra = jnp.where(same_head, _tile_heads(q * jnp.exp(b - b_mid)), 0.0)
        k_intra = _tile_heads((k * jnp.exp(b_mid - b)).astype(_BF16))
        q_dec = jnp.where(same_head, _tile_heads(q * jnp.exp(b)), 0.0)
        k_dec = jnp.where(same_head, _tile_heads(k * jnp.exp(b_last - b)), 0.0)
        state_t = sgla_ref[...]
        att = jnp.where(causal, _dot_nt(q_intra, k_intra), 0.0)
        o_s = _dot(att, v_s) + _dot_nt(q_dec, state_t)
        sgla_ref[...] = state_t * jnp.exp(b_last) + _dot_tn(v_s, k_dec)
        og_s = _stack_heads(proj_ref[rows, C_GLA_OG:C_GLA_OG + GLA_WIDTH], GLA_DV)
        y_s = gated_out(o_s, og_s, glanw_ref[...])
        for hd in range(GLA_HEADS):
            y_ref[pl.ds(r0, CHUNK), hd * LANES:(hd + 1) * LANES] = y_s[hd * CHUNK:(hd + 1) * CHUNK]

        pre = proj_ref[pl.ds(r0, CHUNK + SUBLANES), C_GDN_QKV:C_GDN_QKV + 3 * GDN_WIDTH]
        conv = conv_w[0:1, :] * pre[SUBLANES - 3:SUBLANES - 3 + CHUNK, :]
        for kk in range(1, CONV_WIDTH):
            off = SUBLANES - (CONV_WIDTH - 1) + kk
            conv = conv + conv_w[kk:kk + 1, :] * pre[off:off + CHUNK, :]
        qkv = conv * _sigmoid(conv)
        q_s = _stack_heads(qkv[:, 0:GDN_WIDTH], GDN_DK)
        k_s = _stack_heads(qkv[:, GDN_WIDTH:2 * GDN_WIDTH], GDN_DK)
        v_s = _stack_heads(qkv[:, 2 * GDN_WIDTH:3 * GDN_WIDTH], GDN_DV)
        q_s = q_s * lax.rsqrt(jnp.sum(q_s * q_s, axis=-1, keepdims=True) + RMS_EPS) * (GDN_DK ** -0.5)
        k_s = k_s * lax.rsqrt(jnp.sum(k_s * k_s, axis=-1, keepdims=True) + RMS_EPS)
        beta_s = _stack_cols(beta_all, L_B)
        d_col = _stack_cols(d_all, L_A)
        e_d = _stack_cols(jnp.exp(d_all), L_A)
        e_rest = _stack_cols(jnp.exp(d_all[CHUNK - 1:CHUNK, :] - d_all), L_A)
        d_row = d_rows[L_A:L_A + 1, :]
        for hd in range(1, GDN_HEADS):
            d_row = jnp.where(lane_head == hd, d_rows[L_A + hd:L_A + hd + 1, :], d_row)
        decay = jnp.exp(jnp.where(causal, d_col - d_row, -jnp.inf))
        k_beta = k_s * beta_s
        aq = _dot_nt(jnp.concatenate([k_beta, q_s], axis=0), k_s)
        a_ref[c] = jnp.where(strict, aq[:STACK] * decay, 0.0).astype(_BF16)
        qk_ref[c] = jnp.where(causal, aq[STACK:] * decay, 0.0).astype(_BF16)
        sol_ref[c] = jnp.concatenate([v_s * beta_s, k_beta * e_d], axis=1)
        qd_ref[c] = (q_s * e_d).astype(_BF16)
        kd_ref[c] = (k_s * e_rest).astype(_BF16)
        dl_ref[c] = d_all[CHUNK - SUBLANES:CHUNK, :]
        return carry

    lax.fori_loop(0, n_chunks, prepare, 0)

    def solve(gi, carry):
        base = gi * SOLVE_GROUP
        pows = [a_ref[base + j] for j in range(SOLVE_GROUP)]
        sols = [sol_ref[base + j] for j in range(SOLVE_GROUP)]
        sols = [s - _dot(p, s) for p, s in zip(pows, sols)]
        for _ in range(5):
            pows = [_dot(p, p).astype(_BF16) for p in pows]
            sols = [s + _dot(p, s) for p, s in zip(pows, sols)]
        for j in range(SOLVE_GROUP):
            sol_ref[base + j] = sols[j]
        return carry

    lax.fori_loop(0, n_chunks // SOLVE_GROUP, solve, 0)

    def recur(c, carry):
        r0 = pl.multiple_of(c * CHUNK, CHUNK)
        rows = pl.ds(r0 + SUBLANES, CHUNK)
        sol = sol_ref[c]
        u_s = sol[:, :GDN_DV]
        w_b = sol[:, GDN_DV:].astype(_BF16)
        q_d = qd_ref[c]
        k_d = kd_ref[c]
        d_last = dl_ref[c][SUBLANES - 1:SUBLANES, :]
        states = [sgdn_ref[hd] for hd in range(GDN_HEADS)]
        ws, qs = [], []
        for hd in range(GDN_HEADS):
            hs = slice(hd * CHUNK, (hd + 1) * CHUNK)
            wq = _dot(jnp.concatenate([w_b[hs], q_d[hs]], axis=0), states[hd])
            ws.append(wq[:CHUNK])
            qs.append(wq[CHUNK:])
        v_new = (u_s - jnp.concatenate(ws, axis=0)).astype(_BF16)
        o_s = jnp.concatenate(qs, axis=0) + _dot(qk_ref[c], v_new)
        for hd in range(GDN_HEADS):
            hs = slice(hd * CHUNK, (hd + 1) * CHUNK)
            dec = jnp.exp(d_last[:, L_A + hd:L_A + hd + 1])
            sgdn_ref[hd] = dec * states[hd] + _dot_tn(k_d[hs], v_new[hs])
        og_s = _stack_heads(proj_ref[rows, C_GDN_OG:C_GDN_OG + GDN_WIDTH], GDN_DV)
        y_s = gated_out(o_s, og_s, gdnnw_ref[...])
        for hd in range(GDN_HEADS):
            y_ref[pl.ds(r0, CHUNK), (GLA_HEADS + hd) * LANES:(GLA_HEADS + hd + 1) * LANES] = \
                y_s[hd * CHUNK:(hd + 1) * CHUNK]
        return carry

    lax.fori_loop(0, n_chunks, recur, 0)

    proj_ref[0:SUBLANES, :] = proj_ref[tb:tb + SUBLANES, :]

    y = jnp.dot(y_ref[...], w_out_ref[...], preferred_element_type=_F32)
    r = alpha * x_ref[...] + (1.0 + gate) * y
    mu = jnp.mean(r, axis=-1, keepdims=True)
    rc = r - mu
    var = jnp.mean(rc * rc, axis=-1, keepdims=True)
    out_ref[...] = rc * lax.rsqrt(var + LN_EPS) * lnw_ref[...] + lnb_ref[...]


def _modulation(c, w_ada, b_ada):
    bsz, d = c.shape
    n = w_ada.shape[1]
    return pl.pallas_call(
        _mod_kernel,
        grid=(n // d,),
        in_specs=[pl.BlockSpec((bsz, d), lambda j: (0, 0)),
                  pl.BlockSpec((d, d), lambda j: (0, j)),
                  pl.BlockSpec((1, d), lambda j: (0, j))],
        out_specs=pl.BlockSpec((bsz, d), lambda j: (0, j)),
        out_shape=jax.ShapeDtypeStruct((bsz, n), _F32),
        name="adaln_modulation",
    )(c, w_ada, b_ada.reshape(1, n))


def _pack_weights(w_in, gla_w_gate_up, gdn_a_log, gdn_dt_bias):
    o = 0
    seg = {}
    for name, size in (("gla_q", GLA_QK), ("gla_k", GLA_QK), ("gla_v", GLA_WIDTH), ("gla_lr", GLA_GATE_RANK),
                       ("gla_og", GLA_WIDTH), ("gdn_qkv", 3 * GDN_WIDTH), ("gdn_a", GDN_HEADS),
                       ("gdn_b", GDN_HEADS), ("gdn_og", GDN_WIDTH)):
        seg[name] = w_in[:, o:o + size]
        o += size
    pad = jnp.zeros((w_in.shape[0], LANES - L_B - GDN_HEADS), w_in.dtype)
    w_all = jnp.concatenate([seg["gla_q"], seg["gla_k"], seg["gla_v"], seg["gla_og"], seg["gdn_qkv"],
                             seg["gdn_og"], seg["gla_lr"], seg["gdn_a"], seg["gdn_b"], pad], axis=1)
    wg = jnp.zeros((LANES, GLA_QK), _F32).at[L_LR:L_LR + GLA_GATE_RANK, :].set(gla_w_gate_up)
    alog_row = jnp.zeros((1, LANES), _F32).at[0, L_A:L_A + GDN_HEADS].set(gdn_a_log)
    dtb_row = jnp.zeros((1, LANES), _F32).at[0, L_A:L_A + GDN_HEADS].set(gdn_dt_bias)
    return w_all.astype(_BF16), wg.astype(_BF16), alog_row, dtb_row


def _hybrid_layer(x, c, w_ada, b_ada, w_in, gla_w_gate_up, gla_b_gate, gla_norm_w, gdn_conv_w, gdn_a_log,
                  gdn_dt_bias, gdn_norm_w, w_out, ln_w, ln_b, *, alpha):
    bsz, seq, d = x.shape
    tb = min(TIME_BLOCK, seq)
    assert d == D_MODEL and seq % tb == 0 and tb % CHUNK == 0
    mod = _modulation(c, w_ada, b_ada).reshape(bsz, 3, d)
    w_all, wg, alog_row, dtb_row = _pack_weights(w_in, gla_w_gate_up, gdn_a_log, gdn_dt_bias)
    n_chunks = tb // CHUNK
    assert GLA_HEADS == GDN_HEADS and n_chunks % SOLVE_GROUP == 0
    const = lambda b, t: (0, 0)
    return pl.pallas_call(
        functools.partial(_layer_kernel, tb=tb, alpha=alpha),
        grid=(bsz, seq // tb),
        in_specs=[
            pl.BlockSpec((None, tb, d), lambda b, t: (b, t, 0)),
            pl.BlockSpec((None, 3, d), lambda b, t: (b, 0, 0)),
            pl.BlockSpec((d, N_COLS), const),
            pl.BlockSpec((LANES, GLA_QK), const),
            pl.BlockSpec((1, GLA_QK), const),
            pl.BlockSpec((1, GLA_DV), const),
            pl.BlockSpec((1, GDN_DV), const),
            pl.BlockSpec((CONV_WIDTH, 3 * GDN_WIDTH), const),
            pl.BlockSpec((1, LANES), const),
            pl.BlockSpec((1, LANES), const),
            pl.BlockSpec((d, d), const),
            pl.BlockSpec((1, d), const),
            pl.BlockSpec((1, d), const),
        ],
        out_specs=pl.BlockSpec((None, tb, d), lambda b, t: (b, t, 0)),
        out_shape=jax.ShapeDtypeStruct((bsz, seq, d), _F32),
        scratch_shapes=[
            pltpu.VMEM((tb + SUBLANES, N_COLS), _F32),
            pltpu.VMEM((tb, d), _BF16),
            pltpu.VMEM((GLA_DV, GLA_QK), _F32),
            pltpu.VMEM((GDN_HEADS, GDN_DK, GDN_DV), _F32),
            pltpu.VMEM((n_chunks, STACK, STACK), _BF16),
            pltpu.VMEM((n_chunks, STACK, STACK), _BF16),
            pltpu.VMEM((n_chunks, STACK, GDN_DV + GDN_DK), _F32),
            pltpu.VMEM((n_chunks, STACK, GDN_DK), _BF16),
            pltpu.VMEM((n_chunks, STACK, GDN_DK), _BF16),
            pltpu.VMEM((n_chunks, SUBLANES, LANES), _F32),
        ],
        compiler_params=pltpu.CompilerParams(dimension_semantics=("arbitrary", "arbitrary"),
                                             vmem_limit_bytes=VMEM_LIMIT_BYTES),
        name="hybrid_layer",
    )(x, mod, w_all, wg, gla_b_gate.reshape(1, -1), gla_norm_w.reshape(1, -1), gdn_norm_w.reshape(1, -1),
      gdn_conv_w, alog_row, dtb_row,
      w_out.astype(_BF16), ln_w.reshape(1, -1), ln_b.reshape(1, -1))


def kernel(x, c, w_ada, b_ada, w_in, gla_w_gate_up, gla_b_gate, gla_norm_w, gdn_conv_w, gdn_a_log, gdn_dt_bias, gdn_norm_w, w_out, ln_w, ln_b):
    depth = w_ada.shape[0]
    alpha = (2.0 * depth) ** 0.25
    for layer in range(depth):
        x = _hybrid_layer(x, c, w_ada[layer], b_ada[layer], w_in[layer], gla_w_gate_up[layer],
                          gla_b_gate[layer], gla_norm_w[layer], gdn_conv_w[layer], gdn_a_log[layer],
                          gdn_dt_bias[layer], gdn_norm_w[layer], w_out[layer], ln_w[layer], ln_b[layer],
                          alpha=alpha)
    return x
```

```python
import functools

import jax
import jax.numpy as jnp
from jax import lax
from jax.experimental import pallas as pl
from jax.experimental.pallas import tpu as pltpu

D_MODEL = 1024
GLA_HEADS = 4
GLA_WIDTH = 512
GLA_DV = 128
GLA_DK = 64
GLA_QK = GLA_HEADS * GLA_DK
GLA_GATE_RANK = 16
GLA_GATE_NORM = 16.0
GDN_HEADS = 4
GDN_WIDTH = 512
GDN_DK = 128
GDN_DV = 128
CONV_WIDTH = 4
CHUNK = 64
LN_EPS = 1e-5
RMS_EPS = 1e-6

SUBLANES = 8
LANES = 128

C_GLA_Q = 0
C_GLA_K = C_GLA_Q + GLA_QK
C_GLA_V = C_GLA_K + GLA_QK
C_GLA_OG = C_GLA_V + GLA_WIDTH
C_GDN_QKV = C_GLA_OG + GLA_WIDTH
C_GDN_OG = C_GDN_QKV + 3 * GDN_WIDTH
C_SMALL = C_GDN_OG + GDN_WIDTH
N_COLS = C_SMALL + LANES
L_LR = 0
L_A = GLA_GATE_RANK
L_B = L_A + GDN_HEADS

TIME_BLOCK = 512
STACK = GLA_HEADS * CHUNK
SOLVE_GROUP = 4
VMEM_LIMIT_BYTES = 56 * 1024 * 1024

_F32 = jnp.float32
_BF16 = jnp.bfloat16


def _dot(a, b):
    return jnp.dot(a.astype(_BF16), b.astype(_BF16), preferred_element_type=_F32)


def _dot_nt(a, b):
    return lax.dot_general(a.astype(_BF16), b.astype(_BF16), (((1,), (1,)), ((), ())),
                           preferred_element_type=_F32)


def _dot_tn(a, b):
    return lax.dot_general(a.astype(_BF16), b.astype(_BF16), (((0,), (0,)), ((), ())),
                           preferred_element_type=_F32)


def _split_hi_lo(a):
    hi = a.astype(_BF16)
    lo = (a - hi.astype(_F32)).astype(_BF16)
    return hi, lo


def _sigmoid(a):
    return 1.0 / (1.0 + jnp.exp(-a))


def _softplus(a):
    return jnp.maximum(a, 0.0) + jnp.log1p(jnp.exp(-jnp.abs(a)))


def _mod_kernel(c_ref, w_ref, b_ref, o_ref):
    o_ref[...] = jnp.dot(c_ref[...], w_ref[...], preferred_element_type=_F32,
                         precision=lax.Precision.HIGHEST) + b_ref[...]


def _tile_heads(a):
    return jnp.concatenate([a] * GLA_HEADS, axis=0)


def _stack_heads(a, width):
    return jnp.concatenate([a[:, hd * width:(hd + 1) * width] for hd in range(GLA_HEADS)], axis=0)


def _stack_cols(a, lane0):
    return jnp.concatenate([a[:, lane0 + hd:lane0 + hd + 1] for hd in range(GDN_HEADS)], axis=0)


def _layer_kernel(x_ref, mod_ref, w_all_ref, wg_ref, bg_ref, glanw_ref, gdnnw_ref, convw_ref, alog_ref, dtb_ref,
                  w_out_ref, lnw_ref, lnb_ref, out_ref, proj_ref, y_ref, sgla_ref, sgdn_ref, a_ref, qk_ref,
                  sol_ref, qd_ref, kd_ref, dl_ref, tail_ref, *, tb, alpha):
    t = pl.program_id(1)
    n_chunks = tb // CHUNK

    @pl.when(t == 0)
    def _():
        proj_ref[0:SUBLANES, :] = jnp.zeros((SUBLANES, N_COLS), _F32)
        sgla_ref[...] = jnp.zeros_like(sgla_ref)
        sgdn_ref[...] = jnp.zeros_like(sgdn_ref)

    shift = mod_ref[0:1, :]
    scale = mod_ref[1:2, :]
    gate = mod_ref[2:3, :]
    h = x_ref[...] * (1.0 + scale) + shift
    proj_ref[SUBLANES:, :] = _dot(h, w_all_ref[...])

    qkv_cols = slice(C_GDN_QKV, C_GDN_QKV + 3 * GDN_WIDTH)
    conv_w = convw_ref[...]
    tail_ref[...] = proj_ref[tb:tb + SUBLANES, qkv_cols]
    for ci in reversed(range(n_chunks)):
        base = SUBLANES + ci * CHUNK
        conv = None
        for kk in range(CONV_WIDTH):
            off = base - (CONV_WIDTH - 1) + kk
            term = conv_w[kk:kk + 1, :] * proj_ref[off:off + CHUNK, qkv_cols]
            conv = term if conv is None else conv + term
        proj_ref[base:base + CHUNK, qkv_cols] = conv * _sigmoid(conv)
    proj_ref[0:SUBLANES, qkv_cols] = tail_ref[...]

    row = lax.broadcasted_iota(jnp.int32, (STACK, STACK), 0)
    col = lax.broadcasted_iota(jnp.int32, (STACK, STACK), 1)
    same_head = (row // CHUNK) == (col // CHUNK)
    causal = same_head & (row >= col)
    strict = same_head & (row > col)
    row_c = lax.broadcasted_iota(jnp.int32, (CHUNK, CHUNK), 0)
    col_c = lax.broadcasted_iota(jnp.int32, (CHUNK, CHUNK), 1)
    tri_lower = (row_c >= col_c).astype(_BF16)
    tri_upper4 = jnp.concatenate([(row_c <= col_c).astype(_BF16)] * GDN_HEADS, axis=1)
    lane_head = lax.broadcasted_iota(jnp.int32, (1, STACK), 1) // CHUNK
    neg_exp_alog = -jnp.exp(alog_ref[...])
    dt_bias = dtb_ref[...]
    bg = bg_ref[...]

    def gated_out(o_s, og_s, nw):
        o_n = o_s * lax.rsqrt(jnp.mean(o_s * o_s, axis=-1, keepdims=True) + RMS_EPS) * nw
        return (o_n * (og_s * _sigmoid(og_s))).astype(_BF16)

    def prepare(c, carry):
        r0 = c * CHUNK
        rows = pl.ds(r0 + SUBLANES, CHUNK)

        small = proj_ref[rows, C_SMALL:C_SMALL + LANES]
        z = _dot(small, wg_ref[...]) + bg
        g_gla = (jnp.minimum(z, 0.0) - jnp.log1p(jnp.exp(-jnp.abs(z)))) * (1.0 / GLA_GATE_NORM)
        g_gdn = neg_exp_alog * _softplus(small + dt_bias)
        beta_all = _sigmoid(small)

        g_all = jnp.concatenate([g_gla, g_gdn], axis=1)
        g_hi, g_lo = _split_hi_lo(g_all)
        cum = (jnp.dot(tri_lower, g_hi, preferred_element_type=_F32)
               + jnp.dot(tri_lower, g_lo, preferred_element_type=_F32))
        b = cum[:, :GLA_QK]
        d_all = cum[:, GLA_QK:]
        gd_hi, gd_lo = _split_hi_lo(g_gdn)
        d_rows = _dot_tn(gd_hi, tri_upper4) + _dot_tn(gd_lo, tri_upper4)

        q = proj_ref[rows, C_GLA_Q:C_GLA_Q + GLA_QK] * (GLA_DK ** -0.5)
        k = proj_ref[rows, C_GLA_K:C_GLA_K + GLA_QK]
        v_s = _stack_heads(proj_ref[rows, C_GLA_V:C_GLA_V + GLA_WIDTH], GLA_DV)
        b_mid = b[CHUNK // 2 - 1:CHUNK // 2, :]
        b_last = b[CHUNK - 1:CHUNK, :]
        q_intra = jnp.where(same_head, _tile_heads(q * jnp.exp(b - b_mid)), 0.0)
        k_intra = _tile_heads((k * jnp.exp(b_mid - b)).astype(_BF16))
        q_dec = jnp.where(same_head, _tile_heads(q * jnp.exp(b)), 0.0)
        k_dec = jnp.where(same_head, _tile_heads(k * jnp.exp(b_last - b)), 0.0)
        state_t = sgla_ref[...]
        att = jnp.where(causal, _dot_nt(q_intra, k_intra), 0.0)
        o_s = _dot(att, v_s) + _dot_nt(q_dec, state_t)
        sgla_ref[...] = state_t * jnp.exp(b_last) + _dot_tn(v_s, k_dec)
        og_s = _stack_heads(proj_ref[rows, C_GLA_OG:C_GLA_OG + GLA_WIDTH], GLA_DV)
        y_s = gated_out(o_s, og_s, glanw_ref[...])
        for hd in range(GLA_HEADS):
            y_ref[pl.ds(r0, CHUNK), hd * LANES:(hd + 1) * LANES] = y_s[hd * CHUNK:(hd + 1) * CHUNK]

        qkv = proj_ref[rows, C_GDN_QKV:C_GDN_QKV + 3 * GDN_WIDTH]
        q_s = _stack_heads(qkv[:, 0:GDN_WIDTH], GDN_DK)
        k_s = _stack_heads(qkv[:, GDN_WIDTH:2 * GDN_WIDTH], GDN_DK)
        v_s = _stack_heads(qkv[:, 2 * GDN_WIDTH:3 * GDN_WIDTH], GDN_DV)
        q_s = q_s * lax.rsqrt(jnp.sum(q_s * q_s, axis=-1, keepdims=True) + RMS_EPS) * (GDN_DK ** -0.5)
        k_s = k_s * lax.rsqrt(jnp.sum(k_s * k_s, axis=-1, keepdims=True) + RMS_EPS)
        beta_s = _stack_cols(beta_all, L_B)
        d_col = _stack_cols(d_all, L_A)
        e_d = _stack_cols(jnp.exp(d_all), L_A)
        e_rest = _stack_cols(jnp.exp(d_all[CHUNK - 1:CHUNK, :] - d_all), L_A)
        d_row = d_rows[L_A:L_A + 1, :]
        for hd in range(1, GDN_HEADS):
            d_row = jnp.where(lane_head == hd, d_rows[L_A + hd:L_A + hd + 1, :], d_row)
        decay = jnp.exp(jnp.where(causal, d_col - d_row, -jnp.inf))
        k_beta = k_s * beta_s
        aq = _dot_nt(jnp.concatenate([k_beta, q_s], axis=0), k_s)
        a_ref[c] = jnp.where(strict, aq[:STACK] * decay, 0.0).astype(_BF16)
        qk_ref[c] = jnp.where(causal, aq[STACK:] * decay, 0.0).astype(_BF16)
        sol_ref[c] = jnp.concatenate([v_s * beta_s, k_beta * e_d], axis=1)
        qd_ref[c] = (q_s * e_d).astype(_BF16)
        kd_ref[c] = (k_s * e_rest).astype(_BF16)
        dl_ref[c] = d_all[CHUNK - SUBLANES:CHUNK, :]
        return carry


    def solve(gi, carry):
        base = gi * SOLVE_GROUP
        pows = [a_ref[base + j] for j in range(SOLVE_GROUP)]
        sols = [sol_ref[base + j] for j in range(SOLVE_GROUP)]
        sols = [s - _dot(p, s) for p, s in zip(pows, sols)]
        for _ in range(5):
            pows = [_dot(p, p).astype(_BF16) for p in pows]
            sols = [s + _dot(p, s) for p, s in zip(pows, sols)]
        for j in range(SOLVE_GROUP):
            sol_ref[base + j] = sols[j]
        return carry


    def recur(c, carry):
        r0 = c * CHUNK
        rows = pl.ds(r0 + SUBLANES, CHUNK)
        sol = sol_ref[c]
        u_s = sol[:, :GDN_DV]
        w_b = sol[:, GDN_DV:].astype(_BF16)
        q_d = qd_ref[c]
        k_d = kd_ref[c]
        d_last = dl_ref[c][SUBLANES - 1:SUBLANES, :]
        states = [sgdn_ref[hd] for hd in range(GDN_HEADS)]
        ws, qs = [], []
        for hd in range(GDN_HEADS):
            hs = slice(hd * CHUNK, (hd + 1) * CHUNK)
            wq = _dot(jnp.concatenate([w_b[hs], q_d[hs]], axis=0), states[hd])
            ws.append(wq[:CHUNK])
            qs.append(wq[CHUNK:])
        v_new = (u_s - jnp.concatenate(ws, axis=0)).astype(_BF16)
        o_s = jnp.concatenate(qs, axis=0) + _dot(qk_ref[c], v_new)
        for hd in range(GDN_HEADS):
            hs = slice(hd * CHUNK, (hd + 1) * CHUNK)
            dec = jnp.exp(d_last[:, L_A + hd:L_A + hd + 1])
            sgdn_ref[hd] = dec * states[hd] + _dot_tn(k_d[hs], v_new[hs])
        og_s = _stack_heads(proj_ref[rows, C_GDN_OG:C_GDN_OG + GDN_WIDTH], GDN_DV)
        y_s = gated_out(o_s, og_s, gdnnw_ref[...])
        for hd in range(GDN_HEADS):
            y_ref[pl.ds(r0, CHUNK), (GLA_HEADS + hd) * LANES:(GLA_HEADS + hd + 1) * LANES] = \
                y_s[hd * CHUNK:(hd + 1) * CHUNK]
        return carry

    n_groups = n_chunks // SOLVE_GROUP
    for step in range(n_groups + 2):
        if step < n_groups:
            for j in range(SOLVE_GROUP):
                prepare(step * SOLVE_GROUP + j, 0)
        if 1 <= step <= n_groups:
            solve(step - 1, 0)
        if step >= 2:
            for j in range(SOLVE_GROUP):
                recur((step - 2) * SOLVE_GROUP + j, 0)

    y = jnp.dot(y_ref[...], w_out_ref[...], preferred_element_type=_F32)
    r = alpha * x_ref[...] + (1.0 + gate) * y
    mu = jnp.mean(r, axis=-1, keepdims=True)
    rc = r - mu
    var = jnp.mean(rc * rc, axis=-1, keepdims=True)
    out_ref[...] = rc * lax.rsqrt(var + LN_EPS) * lnw_ref[...] + lnb_ref[...]


def _modulation(c, w_ada, b_ada):
    bsz, d = c.shape
    n = w_ada.shape[1]
    return pl.pallas_call(
        _mod_kernel,
        grid=(n // d,),
        in_specs=[pl.BlockSpec((bsz, d), lambda j: (0, 0)),
                  pl.BlockSpec((d, d), lambda j: (0, j)),
                  pl.BlockSpec((1, d), lambda j: (0, j))],
        out_specs=pl.BlockSpec((bsz, d), lambda j: (0, j)),
        out_shape=jax.ShapeDtypeStruct((bsz, n), _F32),
        name="adaln_modulation",
    )(c, w_ada, b_ada.reshape(1, n))


def _pack_weights(w_in, gla_w_gate_up, gdn_a_log, gdn_dt_bias):
    o = 0
    seg = {}
    for name, size in (("gla_q", GLA_QK), ("gla_k", GLA_QK), ("gla_v", GLA_WIDTH), ("gla_lr", GLA_GATE_RANK),
                       ("gla_og", GLA_WIDTH), ("gdn_qkv", 3 * GDN_WIDTH), ("gdn_a", GDN_HEADS),
                       ("gdn_b", GDN_HEADS), ("gdn_og", GDN_WIDTH)):
        seg[name] = w_in[:, o:o + size]
        o += size
    pad = jnp.zeros((w_in.shape[0], LANES - L_B - GDN_HEADS), w_in.dtype)
    w_all = jnp.concatenate([seg["gla_q"], seg["gla_k"], seg["gla_v"], seg["gla_og"], seg["gdn_qkv"],
                             seg["gdn_og"], seg["gla_lr"], seg["gdn_a"], seg["gdn_b"], pad], axis=1)
    wg = jnp.zeros((LANES, GLA_QK), _F32).at[L_LR:L_LR + GLA_GATE_RANK, :].set(gla_w_gate_up)
    alog_row = jnp.zeros((1, LANES), _F32).at[0, L_A:L_A + GDN_HEADS].set(gdn_a_log)
    dtb_row = jnp.zeros((1, LANES), _F32).at[0, L_A:L_A + GDN_HEADS].set(gdn_dt_bias)
    return w_all.astype(_BF16), wg.astype(_BF16), alog_row, dtb_row


def _hybrid_layer(x, c, w_ada, b_ada, w_in, gla_w_gate_up, gla_b_gate, gla_norm_w, gdn_conv_w, gdn_a_log,
                  gdn_dt_bias, gdn_norm_w, w_out, ln_w, ln_b, *, alpha):
    bsz, seq, d = x.shape
    tb = min(TIME_BLOCK, seq)
    assert d == D_MODEL and seq % tb == 0 and tb % CHUNK == 0
    mod = _modulation(c, w_ada, b_ada).reshape(bsz, 3, d)
    w_all, wg, alog_row, dtb_row = _pack_weights(w_in, gla_w_gate_up, gdn_a_log, gdn_dt_bias)
    n_chunks = tb // CHUNK
    assert GLA_HEADS == GDN_HEADS and n_chunks % SOLVE_GROUP == 0
    const = lambda b, t: (0, 0)
    return pl.pallas_call(
        functools.partial(_layer_kernel, tb=tb, alpha=alpha),
        grid=(bsz, seq // tb),
        in_specs=[
            pl.BlockSpec((None, tb, d), lambda b, t: (b, t, 0)),
            pl.BlockSpec((None, 3, d), lambda b, t: (b, 0, 0)),
            pl.BlockSpec((d, N_COLS), const),
            pl.BlockSpec((LANES, GLA_QK), const),
            pl.BlockSpec((1, GLA_QK), const),
            pl.BlockSpec((1, GLA_DV), const),
            pl.BlockSpec((1, GDN_DV), const),
            pl.BlockSpec((CONV_WIDTH, 3 * GDN_WIDTH), const),
            pl.BlockSpec((1, LANES), const),
            pl.BlockSpec((1, LANES), const),
            pl.BlockSpec((d, d), const),
            pl.BlockSpec((1, d), const),
            pl.BlockSpec((1, d), const),
        ],
        out_specs=pl.BlockSpec((None, tb, d), lambda b, t: (b, t, 0)),
        out_shape=jax.ShapeDtypeStruct((bsz, seq, d), _F32),
        scratch_shapes=[
            pltpu.VMEM((tb + SUBLANES, N_COLS), _F32),
            pltpu.VMEM((tb, d), _BF16),
            pltpu.VMEM((GLA_DV, GLA_QK), _F32),
            pltpu.VMEM((GDN_HEADS, GDN_DK, GDN_DV), _F32),
            pltpu.VMEM((n_chunks, STACK, STACK), _BF16),
            pltpu.VMEM((n_chunks, STACK, STACK), _BF16),
            pltpu.VMEM((n_chunks, STACK, GDN_DV + GDN_DK), _F32),
            pltpu.VMEM((n_chunks, STACK, GDN_DK), _BF16),
            pltpu.VMEM((n_chunks, STACK, GDN_DK), _BF16),
            pltpu.VMEM((n_chunks, SUBLANES, LANES), _F32),
            pltpu.VMEM((SUBLANES, 3 * GDN_WIDTH), _F32),
        ],
        compiler_params=pltpu.CompilerParams(dimension_semantics=("arbitrary", "arbitrary"),
                                             vmem_limit_bytes=VMEM_LIMIT_BYTES),
        name="hybrid_layer",
    )(x, mod, w_all, wg, gla_b_gate.reshape(1, -1), gla_norm_w.reshape(1, -1), gdn_norm_w.reshape(1, -1),
      gdn_conv_w, alog_row, dtb_row,
      w_out.astype(_BF16), ln_w.reshape(1, -1), ln_b.reshape(1, -1))


def kernel(x, c, w_ada, b_ada, w_in, gla_w_gate_up, gla_b_gate, gla_norm_w, gdn_conv_w, gdn_a_log, gdn_dt_bias, gdn_norm_w, w_out, ln_w, ln_b):
    depth = w_ada.shape[0]
    alpha = (2.0 * depth) ** 0.25
    for layer in range(depth):
        x = _hybrid_layer(x, c, w_ada[layer], b_ada[layer], w_in[layer], gla_w_gate_up[layer],
                          gla_b_gate[layer], gla_norm_w[layer], gdn_conv_w[layer], gdn_a_log[layer],
                          gdn_dt_bias[layer], gdn_norm_w[layer], w_out[layer], ln_w[layer], ln_b[layer],
                          alpha=alpha)
    return x
```

```python
import functools

import jax
import jax.numpy as jnp
from jax import lax
from jax.experimental import pallas as pl
from jax.experimental.pallas import tpu as pltpu

D_MODEL = 1024
GLA_HEADS = 4
GLA_WIDTH = 512
GLA_DV = 128
GLA_DK = 64
GLA_QK = GLA_HEADS * GLA_DK
GLA_GATE_RANK = 16
GLA_GATE_NORM = 16.0
GDN_HEADS = 4
GDN_WIDTH = 512
GDN_DK = 128
GDN_DV = 128
CONV_WIDTH = 4
CHUNK = 64
LN_EPS = 1e-5
RMS_EPS = 1e-6

SUBLANES = 8
LANES = 128

C_GLA_Q = 0
C_GLA_K = C_GLA_Q + GLA_QK
C_GLA_V = C_GLA_K + GLA_QK
C_GLA_OG = C_GLA_V + GLA_WIDTH
C_GDN_QKV = C_GLA_OG + GLA_WIDTH
C_GDN_OG = C_GDN_QKV + 3 * GDN_WIDTH
C_SMALL = C_GDN_OG + GDN_WIDTH
N_COLS = C_SMALL + LANES
L_LR = 0
L_A = GLA_GATE_RANK
L_B = L_A + GDN_HEADS

TIME_BLOCK = 512
STACK = GLA_HEADS * CHUNK
SOLVE_GROUP = 4
PROJ_PIECE = 1024
FINISH_ROWS = 128
VMEM_LIMIT_BYTES = 56 * 1024 * 1024

_F32 = jnp.float32
_BF16 = jnp.bfloat16


def _dot(a, b):
    return jnp.dot(a.astype(_BF16), b.astype(_BF16), preferred_element_type=_F32)


def _dot_nt(a, b):
    return lax.dot_general(a.astype(_BF16), b.astype(_BF16), (((1,), (1,)), ((), ())),
                           preferred_element_type=_F32)


def _dot_tn(a, b):
    return lax.dot_general(a.astype(_BF16), b.astype(_BF16), (((0,), (0,)), ((), ())),
                           preferred_element_type=_F32)


def _split_hi_lo(a):
    hi = a.astype(_BF16)
    lo = (a - hi.astype(_F32)).astype(_BF16)
    return hi, lo


def _sigmoid(a):
    return 1.0 / (1.0 + jnp.exp(-a))


def _softplus(a):
    return jnp.maximum(a, 0.0) + jnp.log1p(jnp.exp(-jnp.abs(a)))


def _mod_kernel(c_ref, w_ref, b_ref, o_ref):
    o_ref[...] = jnp.dot(c_ref[...], w_ref[...], preferred_element_type=_F32,
                         precision=lax.Precision.HIGHEST) + b_ref[...]


def _tile_heads(a):
    return jnp.concatenate([a] * GLA_HEADS, axis=0)


def _stack_heads(a, width):
    return jnp.concatenate([a[:, hd * width:(hd + 1) * width] for hd in range(GLA_HEADS)], axis=0)


def _stack_cols(a, lane0):
    return jnp.concatenate([a[:, lane0 + hd:lane0 + hd + 1] for hd in range(GDN_HEADS)], axis=0)


def _layer_kernel(x_ref, mod_ref, w_all_ref, wg_ref, bg_ref, glanw_ref, gdnnw_ref, convw_ref, alog_ref, dtb_ref,
                  w_out_ref, lnw_ref, lnb_ref, out_ref, proj_ref, y_ref, sgla_ref, sgdn_ref, a_ref, qk_ref,
                  sol_ref, qd_ref, kd_ref, dl_ref, qkv_ref, hb_ref, *, tb, alpha):
    t = pl.program_id(1)
    n_chunks = tb // CHUNK

    @pl.when(t == 0)
    def _():
        proj_ref[0:SUBLANES, :] = jnp.zeros((SUBLANES, N_COLS), _F32)
        sgla_ref[...] = jnp.zeros_like(sgla_ref)
        sgdn_ref[...] = jnp.zeros_like(sgdn_ref)

    qkv_cols = slice(C_GDN_QKV, C_GDN_QKV + 3 * GDN_WIDTH)

    @pl.when(t > 0)
    def _():
        proj_ref[0:SUBLANES, qkv_cols] = proj_ref[tb:tb + SUBLANES, qkv_cols]

    shift = mod_ref[0:1, :]
    scale = mod_ref[1:2, :]
    gate = mod_ref[2:3, :]
    conv_w = convw_ref[...]
    group_rows = SOLVE_GROUP * CHUNK

    def project(gi):
        rs = slice(gi * group_rows, (gi + 1) * group_rows)
        hb_ref[...] = (x_ref[rs, :] * (1.0 + scale) + shift).astype(_BF16)
        for c0 in range(0, N_COLS, PROJ_PIECE):
            c1 = min(c0 + PROJ_PIECE, N_COLS)
            proj_ref[SUBLANES + gi * group_rows:SUBLANES + (gi + 1) * group_rows, c0:c1] = jnp.dot(
                hb_ref[...], w_all_ref[:, c0:c1], preferred_element_type=_F32)
            yield

    def conv_silu(c):
        base = SUBLANES + c * CHUNK
        conv = None
        for kk in range(CONV_WIDTH):
            off = base - (CONV_WIDTH - 1) + kk
            term = conv_w[kk:kk + 1, :] * proj_ref[off:off + CHUNK, qkv_cols]
            conv = term if conv is None else conv + term
        qkv_ref[c * CHUNK:(c + 1) * CHUNK, :] = conv * _sigmoid(conv)

    row = lax.broadcasted_iota(jnp.int32, (STACK, STACK), 0)
    col = lax.broadcasted_iota(jnp.int32, (STACK, STACK), 1)
    same_head = (row // CHUNK) == (col // CHUNK)
    causal = same_head & (row >= col)
    strict = same_head & (row > col)
    row_c = lax.broadcasted_iota(jnp.int32, (CHUNK, CHUNK), 0)
    col_c = lax.broadcasted_iota(jnp.int32, (CHUNK, CHUNK), 1)
    tri_lower = (row_c >= col_c).astype(_BF16)
    tri_upper4 = jnp.concatenate([(row_c <= col_c).astype(_BF16)] * GDN_HEADS, axis=1)
    lane_head = lax.broadcasted_iota(jnp.int32, (1, STACK), 1) // CHUNK
    neg_exp_alog = -jnp.exp(alog_ref[...])
    dt_bias = dtb_ref[...]
    bg = bg_ref[...]

    def gated_out(o_s, og_s, nw):
        o_n = o_s * lax.rsqrt(jnp.mean(o_s * o_s, axis=-1, keepdims=True) + RMS_EPS) * nw
        return (o_n * (og_s * _sigmoid(og_s))).astype(_BF16)

    def prepare(c, carry):
        r0 = c * CHUNK
        rows = pl.ds(r0 + SUBLANES, CHUNK)

        small = proj_ref[rows, C_SMALL:C_SMALL + LANES]
        z = _dot(small, wg_ref[...]) + bg
        g_gla = (jnp.minimum(z, 0.0) - jnp.log1p(jnp.exp(-jnp.abs(z)))) * (1.0 / GLA_GATE_NORM)
        g_gdn = neg_exp_alog * _softplus(small + dt_bias)
        beta_all = _sigmoid(small)

        g_all = jnp.concatenate([g_gla, g_gdn], axis=1)
        g_hi, g_lo = _split_hi_lo(g_all)
        cum = (jnp.dot(tri_lower, g_hi, preferred_element_type=_F32)
               + jnp.dot(tri_lower, g_lo, preferred_element_type=_F32))
        b = cum[:, :GLA_QK]
        d_all = cum[:, GLA_QK:]
        gd_hi, gd_lo = _split_hi_lo(g_gdn)
        d_rows = _dot_tn(gd_hi, tri_upper4) + _dot_tn(gd_lo, tri_upper4)

        q = proj_ref[rows, C_GLA_Q:C_GLA_Q + GLA_QK] * (GLA_DK ** -0.5)
        k = proj_ref[rows, C_GLA_K:C_GLA_K + GLA_QK]
        v_s = _stack_heads(proj_ref[rows, C_GLA_V:C_GLA_V + GLA_WIDTH], GLA_DV)
        b_mid = b[CHUNK // 2 - 1:CHUNK // 2, :]
        b_last = b[CHUNK - 1:CHUNK, :]
        q_intra = jnp.where(same_head, _tile_heads(q * jnp.exp(b - b_mid)), 0.0)
        k_intra = _tile_heads((k * jnp.exp(b_mid - b)).astype(_BF16))
        q_dec = jnp.where(same_head, _tile_heads(q * jnp.exp(b)), 0.0)
        k_dec = jnp.where(same_head, _tile_heads(k * jnp.exp(b_last - b)), 0.0)
        state_t = sgla_ref[...]
        att = jnp.where(causal, _dot_nt(q_intra, k_intra), 0.0)
        o_s = _dot(att, v_s) + _dot_nt(q_dec, state_t)
        sgla_ref[...] = state_t * jnp.exp(b_last) + _dot_tn(v_s, k_dec)
        og_s = _stack_heads(proj_ref[rows, C_GLA_OG:C_GLA_OG + GLA_WIDTH], GLA_DV)
        y_s = gated_out(o_s, og_s, glanw_ref[...])
        for hd in range(GLA_HEADS):
            y_ref[pl.ds(r0, CHUNK), hd * LANES:(hd + 1) * LANES] = y_s[hd * CHUNK:(hd + 1) * CHUNK]

        qkv = qkv_ref[pl.ds(r0, CHUNK), :]
        q_s = _stack_heads(qkv[:, 0:GDN_WIDTH], GDN_DK)
        k_s = _stack_heads(qkv[:, GDN_WIDTH:2 * GDN_WIDTH], GDN_DK)
        v_s = _stack_heads(qkv[:, 2 * GDN_WIDTH:3 * GDN_WIDTH], GDN_DV)
        q_s = q_s * lax.rsqrt(jnp.sum(q_s * q_s, axis=-1, keepdims=True) + RMS_EPS) * (GDN_DK ** -0.5)
        k_s = k_s * lax.rsqrt(jnp.sum(k_s * k_s, axis=-1, keepdims=True) + RMS_EPS)
        beta_s = _stack_cols(beta_all, L_B)
        d_col = _stack_cols(d_all, L_A)
        e_d = _stack_cols(jnp.exp(d_all), L_A)
        e_rest = _stack_cols(jnp.exp(d_all[CHUNK - 1:CHUNK, :] - d_all), L_A)
        d_row = d_rows[L_A:L_A + 1, :]
        for hd in range(1, GDN_HEADS):
            d_row = jnp.where(lane_head == hd, d_rows[L_A + hd:L_A + hd + 1, :], d_row)
        decay = jnp.exp(jnp.where(causal, d_col - d_row, -jnp.inf))
        k_beta = k_s * beta_s
        aq = _dot_nt(jnp.concatenate([k_beta, q_s], axis=0), k_s)
        a_ref[c] = jnp.where(strict, aq[:STACK] * decay, 0.0).astype(_BF16)
        qk_ref[c] = jnp.where(causal, aq[STACK:] * decay, 0.0).astype(_BF16)
        sol_ref[c] = jnp.concatenate([v_s * beta_s, k_beta * e_d], axis=1)
        qd_ref[c] = (q_s * e_d).astype(_BF16)
        kd_ref[c] = (k_s * e_rest).astype(_BF16)
        dl_ref[c] = d_all[CHUNK - SUBLANES:CHUNK, :]
        return carry


    def solve(gi):
        base = gi * SOLVE_GROUP
        pows = [a_ref[base + j] for j in range(SOLVE_GROUP)]
        sols = [sol_ref[base + j] for j in range(SOLVE_GROUP)]
        sols = [s - _dot(p, s) for p, s in zip(pows, sols)]
        yield
        for lvl in range(5):
            pows = [_dot(p, p).astype(_BF16) for p in pows]
            sols = [s + _dot(p, s) for p, s in zip(pows, sols)]
            if lvl < 4:
                yield
        for j in range(SOLVE_GROUP):
            sol_ref[base + j] = sols[j]
        yield


    def recur(c, carry):
        r0 = c * CHUNK
        rows = pl.ds(r0 + SUBLANES, CHUNK)
        sol = sol_ref[c]
        u_s = sol[:, :GDN_DV]
        w_b = sol[:, GDN_DV:].astype(_BF16)
        q_d = qd_ref[c]
        k_d = kd_ref[c]
        d_last = dl_ref[c][SUBLANES - 1:SUBLANES, :]
        states = [sgdn_ref[hd] for hd in range(GDN_HEADS)]
        ws, qs = [], []
        for hd in range(GDN_HEADS):
            hs = slice(hd * CHUNK, (hd + 1) * CHUNK)
            wq = _dot(jnp.concatenate([w_b[hs], q_d[hs]], axis=0), states[hd])
            ws.append(wq[:CHUNK])
            qs.append(wq[CHUNK:])
        v_new = (u_s - jnp.concatenate(ws, axis=0)).astype(_BF16)
        o_s = jnp.concatenate(qs, axis=0) + _dot(qk_ref[c], v_new)
        for hd in range(GDN_HEADS):
            hs = slice(hd * CHUNK, (hd + 1) * CHUNK)
            dec = jnp.exp(d_last[:, L_A + hd:L_A + hd + 1])
            sgdn_ref[hd] = dec * states[hd] + _dot_tn(k_d[hs], v_new[hs])
        og_s = _stack_heads(proj_ref[rows, C_GDN_OG:C_GDN_OG + GDN_WIDTH], GDN_DV)
        y_s = gated_out(o_s, og_s, gdnnw_ref[...])
        for hd in range(GDN_HEADS):
            y_ref[pl.ds(r0, CHUNK), (GLA_HEADS + hd) * LANES:(GLA_HEADS + hd + 1) * LANES] = \
                y_s[hd * CHUNK:(hd + 1) * CHUNK]
        return carry

    def finish(gi):
        for r0 in range(gi * group_rows, (gi + 1) * group_rows, FINISH_ROWS):
            rs = slice(r0, r0 + FINISH_ROWS)
            y = jnp.dot(y_ref[rs, :], w_out_ref[...], preferred_element_type=_F32)
            r = alpha * x_ref[rs, :] + (1.0 + gate) * y
            mu = jnp.mean(r, axis=-1, keepdims=True)
            rc = r - mu
            var = jnp.mean(rc * rc, axis=-1, keepdims=True)
            out_ref[rs, :] = rc * lax.rsqrt(var + LN_EPS) * lnw_ref[...] + lnb_ref[...]
            yield

    def prepare_group(gi):
        for j in range(SOLVE_GROUP):
            conv_silu(gi * SOLVE_GROUP + j)
            prepare(gi * SOLVE_GROUP + j, 0)
            yield

    def recur_group(gi):
        for j in range(SOLVE_GROUP):
            recur(gi * SOLVE_GROUP + j, 0)
            yield

    def emit(*stages):
        live = list(stages)
        while live:
            for st in list(live):
                if next(st, StopIteration) is StopIteration:
                    live.remove(st)

    n_groups = n_chunks // SOLVE_GROUP
    for step in range(n_groups + 4):
        stages = []
        if step < n_groups:
            stages.append(project(step))
        if 1 <= step <= n_groups:
            stages.append(prepare_group(step - 1))
        if 2 <= step <= n_groups + 1:
            stages.append(solve(step - 2))
        if 3 <= step <= n_groups + 2:
            stages.append(recur_group(step - 3))
        if step >= 4:
            stages.append(finish(step - 4))
        emit(*stages)


def _modulation(c, w_ada, b_ada):
    bsz, d = c.shape
    n = w_ada.shape[1]
    return pl.pallas_call(
        _mod_kernel,
        grid=(n // d,),
        in_specs=[pl.BlockSpec((bsz, d), lambda j: (0, 0)),
                  pl.BlockSpec((d, d), lambda j: (0, j)),
                  pl.BlockSpec((1, d), lambda j: (0, j))],
        out_specs=pl.BlockSpec((bsz, d), lambda j: (0, j)),
        out_shape=jax.ShapeDtypeStruct((bsz, n), _F32),
        name="adaln_modulation",
    )(c, w_ada, b_ada.reshape(1, n))


def _pack_weights(w_in, gla_w_gate_up, gdn_a_log, gdn_dt_bias):
    o = 0
    seg = {}
    for name, size in (("gla_q", GLA_QK), ("gla_k", GLA_QK), ("gla_v", GLA_WIDTH), ("gla_lr", GLA_GATE_RANK),
                       ("gla_og", GLA_WIDTH), ("gdn_qkv", 3 * GDN_WIDTH), ("gdn_a", GDN_HEADS),
                       ("gdn_b", GDN_HEADS), ("gdn_og", GDN_WIDTH)):
        seg[name] = w_in[:, o:o + size]
        o += size
    pad = jnp.zeros((w_in.shape[0], LANES - L_B - GDN_HEADS), w_in.dtype)
    w_all = jnp.concatenate([seg["gla_q"], seg["gla_k"], seg["gla_v"], seg["gla_og"], seg["gdn_qkv"],
                             seg["gdn_og"], seg["gla_lr"], seg["gdn_a"], seg["gdn_b"], pad], axis=1)
    wg = jnp.zeros((LANES, GLA_QK), _F32).at[L_LR:L_LR + GLA_GATE_RANK, :].set(gla_w_gate_up)
    alog_row = jnp.zeros((1, LANES), _F32).at[0, L_A:L_A + GDN_HEADS].set(gdn_a_log)
    dtb_row = jnp.zeros((1, LANES), _F32).at[0, L_A:L_A + GDN_HEADS].set(gdn_dt_bias)
    return w_all.astype(_BF16), wg.astype(_BF16), alog_row, dtb_row


def _hybrid_layer(x, c, w_ada, b_ada, w_in, gla_w_gate_up, gla_b_gate, gla_norm_w, gdn_conv_w, gdn_a_log,
                  gdn_dt_bias, gdn_norm_w, w_out, ln_w, ln_b, *, alpha):
    bsz, seq, d = x.shape
    tb = min(TIME_BLOCK, seq)
    assert d == D_MODEL and seq % tb == 0 and tb % CHUNK == 0
    mod = _modulation(c, w_ada, b_ada).reshape(bsz, 3, d)
    w_all, wg, alog_row, dtb_row = _pack_weights(w_in, gla_w_gate_up, gdn_a_log, gdn_dt_bias)
    n_chunks = tb // CHUNK
    assert GLA_HEADS == GDN_HEADS and n_chunks % SOLVE_GROUP == 0
    const = lambda b, t: (0, 0)
    return pl.pallas_call(
        functools.partial(_layer_kernel, tb=tb, alpha=alpha),
        grid=(bsz, seq // tb),
        in_specs=[
            pl.BlockSpec((None, tb, d), lambda b, t: (b, t, 0)),
            pl.BlockSpec((None, 3, d), lambda b, t: (b, 0, 0)),
            pl.BlockSpec((d, N_COLS), const),
            pl.BlockSpec((LANES, GLA_QK), const),
            pl.BlockSpec((1, GLA_QK), const),
            pl.BlockSpec((1, GLA_DV), const),
            pl.BlockSpec((1, GDN_DV), const),
            pl.BlockSpec((CONV_WIDTH, 3 * GDN_WIDTH), const),
            pl.BlockSpec((1, LANES), const),
            pl.BlockSpec((1, LANES), const),
            pl.BlockSpec((d, d), const),
            pl.BlockSpec((1, d), const),
            pl.BlockSpec((1, d), const),
        ],
        out_specs=pl.BlockSpec((None, tb, d), lambda b, t: (b, t, 0)),
        out_shape=jax.ShapeDtypeStruct((bsz, seq, d), _F32),
        scratch_shapes=[
            pltpu.VMEM((tb + SUBLANES, N_COLS), _F32),
            pltpu.VMEM((tb, d), _BF16),
            pltpu.VMEM((GLA_DV, GLA_QK), _F32),
            pltpu.VMEM((GDN_HEADS, GDN_DK, GDN_DV), _F32),
            pltpu.VMEM((n_chunks, STACK, STACK), _BF16),
            pltpu.VMEM((n_chunks, STACK, STACK), _BF16),
            pltpu.VMEM((n_chunks, STACK, GDN_DV + GDN_DK), _F32),
            pltpu.VMEM((n_chunks, STACK, GDN_DK), _BF16),
            pltpu.VMEM((n_chunks, STACK, GDN_DK), _BF16),
            pltpu.VMEM((n_chunks, SUBLANES, LANES), _F32),
            pltpu.VMEM((tb, 3 * GDN_WIDTH), _F32),
            pltpu.VMEM((SOLVE_GROUP * CHUNK, d), _BF16),
        ],
        compiler_params=pltpu.CompilerParams(dimension_semantics=("arbitrary", "arbitrary"),
                                             vmem_limit_bytes=VMEM_LIMIT_BYTES),
        name="hybrid_layer",
    )(x, mod, w_all, wg, gla_b_gate.reshape(1, -1), gla_norm_w.reshape(1, -1), gdn_norm_w.reshape(1, -1),
      gdn_conv_w, alog_row, dtb_row,
      w_out.astype(_BF16), ln_w.reshape(1, -1), ln_b.reshape(1, -1))


def kernel(x, c, w_ada, b_ada, w_in, gla_w_gate_up, gla_b_gate, gla_norm_w, gdn_conv_w, gdn_a_log, gdn_dt_bias, gdn_norm_w, w_out, ln_w, ln_b):
    depth = w_ada.shape[0]
    alpha = (2.0 * depth) ** 0.25
    for layer in range(depth):
        x = _hybrid_layer(x, c, w_ada[layer], b_ada[layer], w_in[layer], gla_w_gate_up[layer],
                          gla_b_gate[layer], gla_norm_w[layer], gdn_conv_w[layer], gdn_a_log[layer],
                          gdn_dt_bias[layer], gdn_norm_w[layer], w_out[layer], ln_w[layer], ln_b[layer],
                          alpha=alpha)
    return x
```

```python
import functools

import jax
import jax.numpy as jnp
from jax import lax
from jax.experimental import pallas as pl
from jax.experimental.pallas import tpu as pltpu

D_MODEL = 1024
GLA_HEADS = 4
GLA_WIDTH = 512
GLA_DV = 128
GLA_DK = 64
GLA_QK = GLA_HEADS * GLA_DK
GLA_GATE_RANK = 16
GLA_GATE_NORM = 16.0
GDN_HEADS = 4
GDN_WIDTH = 512
GDN_DK = 128
GDN_DV = 128
CONV_WIDTH = 4
CHUNK = 64
LN_EPS = 1e-5
RMS_EPS = 1e-6

SUBLANES = 8
LANES = 128

C_GLA_Q = 0
C_GLA_K = C_GLA_Q + GLA_QK
C_GLA_V = C_GLA_K + GLA_QK
C_GLA_OG = C_GLA_V + GLA_WIDTH
C_GDN_QKV = C_GLA_OG + GLA_WIDTH
C_GDN_OG = C_GDN_QKV + 3 * GDN_WIDTH
C_SMALL = C_GDN_OG + GDN_WIDTH
N_COLS = C_SMALL + LANES
L_LR = 0
L_A = GLA_GATE_RANK
L_B = L_A + GDN_HEADS

TIME_BLOCK = 512
STACK = GLA_HEADS * CHUNK
SOLVE_GROUP = 4
PROJ_PIECE = 1024
SOLVE_SKIP_MIN = 16
FINISH_ROWS = 128
VMEM_LIMIT_BYTES = 56 * 1024 * 1024

_F32 = jnp.float32
_BF16 = jnp.bfloat16


def _dot(a, b):
    return jnp.dot(a.astype(_BF16), b.astype(_BF16), preferred_element_type=_F32)


def _dot_nt(a, b):
    return lax.dot_general(a.astype(_BF16), b.astype(_BF16), (((1,), (1,)), ((), ())),
                           preferred_element_type=_F32)


def _dot_tn(a, b):
    return lax.dot_general(a.astype(_BF16), b.astype(_BF16), (((0,), (0,)), ((), ())),
                           preferred_element_type=_F32)


def _split_hi_lo(a):
    hi = a.astype(_BF16)
    lo = (a - hi.astype(_F32)).astype(_BF16)
    return hi, lo


def _sigmoid(a):
    return 1.0 / (1.0 + jnp.exp(-a))


def _softplus(a):
    return jnp.maximum(a, 0.0) + jnp.log1p(jnp.exp(-jnp.abs(a)))


def _mod_kernel(c_ref, w_ref, b_ref, o_ref):
    o_ref[...] = jnp.dot(c_ref[...], w_ref[...], preferred_element_type=_F32,
                         precision=lax.Precision.HIGHEST) + b_ref[...]


def _tile_heads(a):
    return jnp.concatenate([a] * GLA_HEADS, axis=0)


def _stack_heads(a, width):
    return jnp.concatenate([a[:, hd * width:(hd + 1) * width] for hd in range(GLA_HEADS)], axis=0)


def _stack_cols(a, lane0):
    return jnp.concatenate([a[:, lane0 + hd:lane0 + hd + 1] for hd in range(GDN_HEADS)], axis=0)


def _lower_rows(a, skip):
    if skip == 0:
        return a
    return jnp.concatenate([a[hd * CHUNK + skip:(hd + 1) * CHUNK] for hd in range(GLA_HEADS)], axis=0)


def _square_lower(p, skip):
    if skip == 0:
        return _dot(p, p).astype(_BF16)
    rows = CHUNK - skip
    low = _dot(_lower_rows(p, skip), p).astype(_BF16)
    zeros = jnp.zeros((skip, p.shape[1]), _BF16)
    return jnp.concatenate([piece for hd in range(GLA_HEADS)
                            for piece in (zeros, low[hd * rows:(hd + 1) * rows])], axis=0)


def _add_lower(s, upd, skip):
    if skip == 0:
        return s + upd
    rows = CHUNK - skip
    return jnp.concatenate([piece for hd in range(GLA_HEADS)
                            for piece in (s[hd * CHUNK:hd * CHUNK + skip],
                                          s[hd * CHUNK + skip:(hd + 1) * CHUNK] + upd[hd * rows:(hd + 1) * rows])],
                           axis=0)


def _layer_kernel(x_ref, mod_ref, w_all_ref, wg_ref, bg_ref, glanw_ref, gdnnw_ref, convw_ref, alog_ref, dtb_ref,
                  w_out_ref, lnw_ref, lnb_ref, out_ref, proj_ref, y_ref, sgla_ref, sgdn_ref, a_ref, qk_ref,
                  sol_ref, qd_ref, kd_ref, dl_ref, qkv_ref, hb_ref, *, tb, alpha):
    t = pl.program_id(1)
    n_chunks = tb // CHUNK

    @pl.when(t == 0)
    def _():
        proj_ref[0:SUBLANES, :] = jnp.zeros((SUBLANES, N_COLS), _F32)
        sgla_ref[...] = jnp.zeros_like(sgla_ref)
        sgdn_ref[...] = jnp.zeros_like(sgdn_ref)

    qkv_cols = slice(C_GDN_QKV, C_GDN_QKV + 3 * GDN_WIDTH)

    @pl.when(t > 0)
    def _():
        proj_ref[0:SUBLANES, qkv_cols] = proj_ref[tb:tb + SUBLANES, qkv_cols]

    shift = mod_ref[0:1, :]
    scale = mod_ref[1:2, :]
    gate = mod_ref[2:3, :]
    conv_w = convw_ref[...]
    group_rows = SOLVE_GROUP * CHUNK

    def project(gi):
        rs = slice(gi * group_rows, (gi + 1) * group_rows)
        hb_ref[...] = (x_ref[rs, :] * (1.0 + scale) + shift).astype(_BF16)
        for c0 in range(0, N_COLS, PROJ_PIECE):
            c1 = min(c0 + PROJ_PIECE, N_COLS)
            proj_ref[SUBLANES + gi * group_rows:SUBLANES + (gi + 1) * group_rows, c0:c1] = jnp.dot(
                hb_ref[...], w_all_ref[:, c0:c1], preferred_element_type=_F32)
            yield

    def conv_silu(c):
        base = SUBLANES + c * CHUNK
        conv = None
        for kk in range(CONV_WIDTH):
            off = base - (CONV_WIDTH - 1) + kk
            term = conv_w[kk:kk + 1, :] * proj_ref[off:off + CHUNK, qkv_cols]
            conv = term if conv is None else conv + term
        qkv_ref[c * CHUNK:(c + 1) * CHUNK, :] = conv * _sigmoid(conv)

    row = lax.broadcasted_iota(jnp.int32, (STACK, STACK), 0)
    col = lax.broadcasted_iota(jnp.int32, (STACK, STACK), 1)
    same_head = (row // CHUNK) == (col // CHUNK)
    causal = same_head & (row >= col)
    strict = same_head & (row > col)
    row_c = lax.broadcasted_iota(jnp.int32, (CHUNK, CHUNK), 0)
    col_c = lax.broadcasted_iota(jnp.int32, (CHUNK, CHUNK), 1)
    tri_lower = (row_c >= col_c).astype(_BF16)
    tri_upper4 = jnp.concatenate([(row_c <= col_c).astype(_BF16)] * GDN_HEADS, axis=1)
    lane_head = lax.broadcasted_iota(jnp.int32, (1, STACK), 1) // CHUNK
    neg_exp_alog = -jnp.exp(alog_ref[...])
    dt_bias = dtb_ref[...]
    bg = bg_ref[...]

    def gated_out(o_s, og_s, nw):
        o_n = o_s * lax.rsqrt(jnp.mean(o_s * o_s, axis=-1, keepdims=True) + RMS_EPS) * nw
        return (o_n * (og_s * _sigmoid(og_s))).astype(_BF16)

    def prepare(c, carry):
        r0 = c * CHUNK
        rows = pl.ds(r0 + SUBLANES, CHUNK)

        small = proj_ref[rows, C_SMALL:C_SMALL + LANES]
        z = _dot(small, wg_ref[...]) + bg
        g_gla = (jnp.minimum(z, 0.0) - jnp.log1p(jnp.exp(-jnp.abs(z)))) * (1.0 / GLA_GATE_NORM)
        g_gdn = neg_exp_alog * _softplus(small + dt_bias)
        beta_all = _sigmoid(small)

        g_all = jnp.concatenate([g_gla, g_gdn], axis=1)
        g_hi, g_lo = _split_hi_lo(g_all)
        cum = (jnp.dot(tri_lower, g_hi, preferred_element_type=_F32)
               + jnp.dot(tri_lower, g_lo, preferred_element_type=_F32))
        b = cum[:, :GLA_QK]
        d_all = cum[:, GLA_QK:]
        gd_hi, gd_lo = _split_hi_lo(g_gdn)
        d_rows = _dot_tn(gd_hi, tri_upper4) + _dot_tn(gd_lo, tri_upper4)

        q = proj_ref[rows, C_GLA_Q:C_GLA_Q + GLA_QK] * (GLA_DK ** -0.5)
        k = proj_ref[rows, C_GLA_K:C_GLA_K + GLA_QK]
        v_s = _stack_heads(proj_ref[rows, C_GLA_V:C_GLA_V + GLA_WIDTH], GLA_DV)
        b_mid = b[CHUNK // 2 - 1:CHUNK // 2, :]
        b_last = b[CHUNK - 1:CHUNK, :]
        q_intra = jnp.where(same_head, _tile_heads(q * jnp.exp(b - b_mid)), 0.0)
        k_intra = _tile_heads((k * jnp.exp(b_mid - b)).astype(_BF16))
        q_dec = jnp.where(same_head, _tile_heads(q * jnp.exp(b)), 0.0)
        k_dec = jnp.where(same_head, _tile_heads(k * jnp.exp(b_last - b)), 0.0)
        state_t = sgla_ref[...]
        att = jnp.where(causal, _dot_nt(q_intra, k_intra), 0.0)
        o_s = _dot(att, v_s) + _dot_nt(q_dec, state_t)
        sgla_ref[...] = state_t * jnp.exp(b_last) + _dot_tn(v_s, k_dec)
        og_s = _stack_heads(proj_ref[rows, C_GLA_OG:C_GLA_OG + GLA_WIDTH], GLA_DV)
        y_s = gated_out(o_s, og_s, glanw_ref[...])
        for hd in range(GLA_HEADS):
            y_ref[pl.ds(r0, CHUNK), hd * LANES:(hd + 1) * LANES] = y_s[hd * CHUNK:(hd + 1) * CHUNK]

        qkv = qkv_ref[pl.ds(r0, CHUNK), :]
        q_s = _stack_heads(qkv[:, 0:GDN_WIDTH], GDN_DK)
        k_s = _stack_heads(qkv[:, GDN_WIDTH:2 * GDN_WIDTH], GDN_DK)
        v_s = _stack_heads(qkv[:, 2 * GDN_WIDTH:3 * GDN_WIDTH], GDN_DV)
        q_s = q_s * lax.rsqrt(jnp.sum(q_s * q_s, axis=-1, keepdims=True) + RMS_EPS) * (GDN_DK ** -0.5)
        k_s = k_s * lax.rsqrt(jnp.sum(k_s * k_s, axis=-1, keepdims=True) + RMS_EPS)
        beta_s = _stack_cols(beta_all, L_B)
        d_col = _stack_cols(d_all, L_A)
        e_d = _stack_cols(jnp.exp(d_all), L_A)
        e_rest = _stack_cols(jnp.exp(d_all[CHUNK - 1:CHUNK, :] - d_all), L_A)
        d_row = d_rows[L_A:L_A + 1, :]
        for hd in range(1, GDN_HEADS):
            d_row = jnp.where(lane_head == hd, d_rows[L_A + hd:L_A + hd + 1, :], d_row)
        decay = jnp.exp(jnp.where(causal, d_col - d_row, -jnp.inf))
        k_beta = k_s * beta_s
        aq = _dot_nt(jnp.concatenate([k_beta, q_s], axis=0), k_s)
        a_ref[c] = jnp.where(strict, aq[:STACK] * decay, 0.0).astype(_BF16)
        qk_ref[c] = (aq[STACK:] * decay).astype(_BF16)
        sol_ref[c] = jnp.concatenate([v_s * beta_s, k_beta * e_d], axis=1)
        qd_ref[c] = (q_s * e_d).astype(_BF16)
        kd_ref[c] = (k_s * e_rest).astype(_BF16)
        dl_ref[c] = d_all[CHUNK - SUBLANES:CHUNK, :]
        return carry


    def solve(gi):
        base = gi * SOLVE_GROUP
        pows = [a_ref[base + j] for j in range(SOLVE_GROUP)]
        sols = [sol_ref[base + j] for j in range(SOLVE_GROUP)]
        sols = [s - _dot(p, s) for p, s in zip(pows, sols)]
        yield
        for lvl in range(5):
            skip = 2 ** (lvl + 1) if 2 ** (lvl + 1) >= SOLVE_SKIP_MIN else 0
            pows = [_square_lower(p, skip) for p in pows]
            sols = [_add_lower(s, _dot(_lower_rows(p, skip), s), skip) for p, s in zip(pows, sols)]
            if lvl < 4:
                yield
        for j in range(SOLVE_GROUP):
            sol_ref[base + j] = sols[j]
        yield


    def recur(c, carry):
        r0 = c * CHUNK
        rows = pl.ds(r0 + SUBLANES, CHUNK)
        sol = sol_ref[c]
        u_s = sol[:, :GDN_DV]
        w_b = sol[:, GDN_DV:].astype(_BF16)
        q_d = qd_ref[c]
        k_d = kd_ref[c]
        d_last = dl_ref[c][SUBLANES - 1:SUBLANES, :]
        states = [sgdn_ref[hd] for hd in range(GDN_HEADS)]
        ws, qs = [], []
        for hd in range(GDN_HEADS):
            hs = slice(hd * CHUNK, (hd + 1) * CHUNK)
            wq = _dot(jnp.concatenate([w_b[hs], q_d[hs]], axis=0), states[hd])
            ws.append(wq[:CHUNK])
            qs.append(wq[CHUNK:])
        v_new = (u_s - jnp.concatenate(ws, axis=0)).astype(_BF16)
        o_s = jnp.concatenate(qs, axis=0) + _dot(qk_ref[c], v_new)
        for hd in range(GDN_HEADS):
            hs = slice(hd * CHUNK, (hd + 1) * CHUNK)
            dec = jnp.exp(d_last[:, L_A + hd:L_A + hd + 1])
            sgdn_ref[hd] = dec * states[hd] + _dot_tn(k_d[hs], v_new[hs])
        og_s = _stack_heads(proj_ref[rows, C_GDN_OG:C_GDN_OG + GDN_WIDTH], GDN_DV)
        y_s = gated_out(o_s, og_s, gdnnw_ref[...])
        for hd in range(GDN_HEADS):
            y_ref[pl.ds(r0, CHUNK), (GLA_HEADS + hd) * LANES:(GLA_HEADS + hd + 1) * LANES] = \
                y_s[hd * CHUNK:(hd + 1) * CHUNK]
        return carry

    def finish(gi):
        for r0 in range(gi * group_rows, (gi + 1) * group_rows, FINISH_ROWS):
            rs = slice(r0, r0 + FINISH_ROWS)
            y = jnp.dot(y_ref[rs, :], w_out_ref[...], preferred_element_type=_F32)
            r = alpha * x_ref[rs, :] + (1.0 + gate) * y
            mu = jnp.mean(r, axis=-1, keepdims=True)
            rc = r - mu
            var = jnp.mean(rc * rc, axis=-1, keepdims=True)
            out_ref[rs, :] = rc * lax.rsqrt(var + LN_EPS) * lnw_ref[...] + lnb_ref[...]
            yield

    def prepare_group(gi):
        for j in range(SOLVE_GROUP):
            conv_silu(gi * SOLVE_GROUP + j)
            prepare(gi * SOLVE_GROUP + j, 0)
            yield

    def recur_group(gi):
        for j in range(SOLVE_GROUP):
            recur(gi * SOLVE_GROUP + j, 0)
            yield

    def emit(*stages):
        live = list(stages)
        while live:
            for st in list(live):
                if next(st, StopIteration) is StopIteration:
                    live.remove(st)

    n_groups = n_chunks // SOLVE_GROUP
    for step in range(n_groups + 4):
        stages = []
        if step < n_groups:
            stages.append(project(step))
        if 1 <= step <= n_groups:
            stages.append(prepare_group(step - 1))
        if 2 <= step <= n_groups + 1:
            stages.append(solve(step - 2))
        if 3 <= step <= n_groups + 2:
            stages.append(recur_group(step - 3))
        if step >= 4:
            stages.append(finish(step - 4))
        emit(*stages)


def _modulation(c, w_ada, b_ada):
    bsz, d = c.shape
    n = w_ada.shape[1]
    return pl.pallas_call(
        _mod_kernel,
        grid=(n // d,),
        in_specs=[pl.BlockSpec((bsz, d), lambda j: (0, 0)),
                  pl.BlockSpec((d, d), lambda j: (0, j)),
                  pl.BlockSpec((1, d), lambda j: (0, j))],
        out_specs=pl.BlockSpec((bsz, d), lambda j: (0, j)),
        out_shape=jax.ShapeDtypeStruct((bsz, n), _F32),
        name="adaln_modulation",
    )(c, w_ada, b_ada.reshape(1, n))


def _pack_weights(w_in, gla_w_gate_up, gdn_a_log, gdn_dt_bias):
    o = 0
    seg = {}
    for name, size in (("gla_q", GLA_QK), ("gla_k", GLA_QK), ("gla_v", GLA_WIDTH), ("gla_lr", GLA_GATE_RANK),
                       ("gla_og", GLA_WIDTH), ("gdn_qkv", 3 * GDN_WIDTH), ("gdn_a", GDN_HEADS),
                       ("gdn_b", GDN_HEADS), ("gdn_og", GDN_WIDTH)):
        seg[name] = w_in[:, o:o + size]
        o += size
    pad = jnp.zeros((w_in.shape[0], LANES - L_B - GDN_HEADS), w_in.dtype)
    w_all = jnp.concatenate([seg["gla_q"], seg["gla_k"], seg["gla_v"], seg["gla_og"], seg["gdn_qkv"],
                             seg["gdn_og"], seg["gla_lr"], seg["gdn_a"], seg["gdn_b"], pad], axis=1)
    wg = jnp.zeros((LANES, GLA_QK), _F32).at[L_LR:L_LR + GLA_GATE_RANK, :].set(gla_w_gate_up)
    alog_row = jnp.zeros((1, LANES), _F32).at[0, L_A:L_A + GDN_HEADS].set(gdn_a_log)
    dtb_row = jnp.zeros((1, LANES), _F32).at[0, L_A:L_A + GDN_HEADS].set(gdn_dt_bias)
    return w_all.astype(_BF16), wg.astype(_BF16), alog_row, dtb_row


def _hybrid_layer(x, c, w_ada, b_ada, w_in, gla_w_gate_up, gla_b_gate, gla_norm_w, gdn_conv_w, gdn_a_log,
                  gdn_dt_bias, gdn_norm_w, w_out, ln_w, ln_b, *, alpha):
    bsz, seq, d = x.shape
    tb = min(TIME_BLOCK, seq)
    assert d == D_MODEL and seq % tb == 0 and tb % CHUNK == 0
    mod = _modulation(c, w_ada, b_ada).reshape(bsz, 3, d)
    w_all, wg, alog_row, dtb_row = _pack_weights(w_in, gla_w_gate_up, gdn_a_log, gdn_dt_bias)
    n_chunks = tb // CHUNK
    assert GLA_HEADS == GDN_HEADS and n_chunks % SOLVE_GROUP == 0
    const = lambda b, t: (0, 0)
    return pl.pallas_call(
        functools.partial(_layer_kernel, tb=tb, alpha=alpha),
        grid=(bsz, seq // tb),
        in_specs=[
            pl.BlockSpec((None, tb, d), lambda b, t: (b, t, 0)),
            pl.BlockSpec((None, 3, d), lambda b, t: (b, 0, 0)),
            pl.BlockSpec((d, N_COLS), const),
            pl.BlockSpec((LANES, GLA_QK), const),
            pl.BlockSpec((1, GLA_QK), const),
            pl.BlockSpec((1, GLA_DV), const),
            pl.BlockSpec((1, GDN_DV), const),
            pl.BlockSpec((CONV_WIDTH, 3 * GDN_WIDTH), const),
            pl.BlockSpec((1, LANES), const),
            pl.BlockSpec((1, LANES), const),
            pl.BlockSpec((d, d), const),
            pl.BlockSpec((1, d), const),
            pl.BlockSpec((1, d), const),
        ],
        out_specs=pl.BlockSpec((None, tb, d), lambda b, t: (b, t, 0)),
        out_shape=jax.ShapeDtypeStruct((bsz, seq, d), _F32),
        scratch_shapes=[
            pltpu.VMEM((tb + SUBLANES, N_COLS), _F32),
            pltpu.VMEM((tb, d), _BF16),
            pltpu.VMEM((GLA_DV, GLA_QK), _F32),
            pltpu.VMEM((GDN_HEADS, GDN_DK, GDN_DV), _F32),
            pltpu.VMEM((n_chunks, STACK, STACK), _BF16),
            pltpu.VMEM((n_chunks, STACK, STACK), _BF16),
            pltpu.VMEM((n_chunks, STACK, GDN_DV + GDN_DK), _F32),
            pltpu.VMEM((n_chunks, STACK, GDN_DK), _BF16),
            pltpu.VMEM((n_chunks, STACK, GDN_DK), _BF16),
            pltpu.VMEM((n_chunks, SUBLANES, LANES), _F32),
            pltpu.VMEM((tb, 3 * GDN_WIDTH), _F32),
            pltpu.VMEM((SOLVE_GROUP * CHUNK, d), _BF16),
        ],
        compiler_params=pltpu.CompilerParams(dimension_semantics=("arbitrary", "arbitrary"),
                                             vmem_limit_bytes=VMEM_LIMIT_BYTES),
        name="hybrid_layer",
    )(x, mod, w_all, wg, gla_b_gate.reshape(1, -1), gla_norm_w.reshape(1, -1), gdn_norm_w.reshape(1, -1),
      gdn_conv_w, alog_row, dtb_row,
      w_out.astype(_BF16), ln_w.reshape(1, -1), ln_b.reshape(1, -1))


def kernel(x, c, w_ada, b_ada, w_in, gla_w_gate_up, gla_b_gate, gla_norm_w, gdn_conv_w, gdn_a_log, gdn_dt_bias, gdn_norm_w, w_out, ln_w, ln_b):
    depth = w_ada.shape[0]
    alpha = (2.0 * depth) ** 0.25
    for layer in range(depth):
        x = _hybrid_layer(x, c, w_ada[layer], b_ada[layer], w_in[layer], gla_w_gate_up[layer],
                          gla_b_gate[layer], gla_norm_w[layer], gdn_conv_w[layer], gdn_a_log[layer],
                          gdn_dt_bias[layer], gdn_norm_w[layer], w_out[layer], ln_w[layer], ln_b[layer],
                          alpha=alpha)
    return x
```

```python
import functools

import jax
import jax.numpy as jnp
from jax import lax
from jax.experimental import pallas as pl
from jax.experimental.pallas import tpu as pltpu

D_MODEL = 1024
GLA_HEADS = 4
GLA_WIDTH = 512
GLA_DV = 128
GLA_DK = 64
GLA_QK = GLA_HEADS * GLA_DK
GLA_GATE_RANK = 16
GLA_GATE_NORM = 16.0
GDN_HEADS = 4
GDN_WIDTH = 512
GDN_DK = 128
GDN_DV = 128
CONV_WIDTH = 4
CHUNK = 64
LN_EPS = 1e-5
RMS_EPS = 1e-6

SUBLANES = 8
LANES = 128

C_GLA_Q = 0
C_GLA_K = C_GLA_Q + GLA_QK
C_GLA_V = C_GLA_K + GLA_QK
C_GLA_OG = C_GLA_V + GLA_WIDTH
C_GDN_QKV = C_GLA_OG + GLA_WIDTH
C_GDN_OG = C_GDN_QKV + 3 * GDN_WIDTH
C_SMALL = C_GDN_OG + GDN_WIDTH
N_COLS = C_SMALL + LANES
L_LR = 0
L_A = GLA_GATE_RANK
L_B = L_A + GDN_HEADS

TIME_BLOCK = 512
STACK = GLA_HEADS * CHUNK
SOLVE_GROUP = 4
PROJ_PIECE = 2048
SOLVE_SKIP_MIN = 2 * SUBLANES
FINISH_ROWS = 128
VMEM_LIMIT_BYTES = 56 * 1024 * 1024

_F32 = jnp.float32
_BF16 = jnp.bfloat16


def _dot(a, b):
    return jnp.dot(a.astype(_BF16), b.astype(_BF16), preferred_element_type=_F32)


def _dot_nt(a, b):
    return lax.dot_general(a.astype(_BF16), b.astype(_BF16), (((1,), (1,)), ((), ())),
                           preferred_element_type=_F32)


def _dot_tn(a, b):
    return lax.dot_general(a.astype(_BF16), b.astype(_BF16), (((0,), (0,)), ((), ())),
                           preferred_element_type=_F32)


def _split_hi_lo(a):
    hi = a.astype(_BF16)
    lo = (a - hi.astype(_F32)).astype(_BF16)
    return hi, lo


def _sigmoid(a):
    return 1.0 / (1.0 + jnp.exp(-a))


def _softplus(a):
    return jnp.maximum(a, 0.0) + jnp.log1p(jnp.exp(-jnp.abs(a)))


def _mod_kernel(c_ref, w_ref, b_ref, o_ref):
    o_ref[...] = jnp.dot(c_ref[...], w_ref[...], preferred_element_type=_F32,
                         precision=lax.Precision.HIGHEST) + b_ref[...]


def _tile_heads(a):
    return jnp.concatenate([a] * GLA_HEADS, axis=0)


def _stack_heads(a, width):
    return jnp.concatenate([a[:, hd * width:(hd + 1) * width] for hd in range(GLA_HEADS)], axis=0)


def _stack_cols(a, lane0):
    return jnp.concatenate([a[:, lane0 + hd:lane0 + hd + 1] for hd in range(GDN_HEADS)], axis=0)


def _lower_rows(a, skip):
    if skip == 0:
        return a
    return jnp.concatenate([a[hd * CHUNK + skip:(hd + 1) * CHUNK] for hd in range(GLA_HEADS)], axis=0)


def _square_lower(p, skip):
    if skip == 0:
        return _dot(p, p).astype(_BF16)
    rows = CHUNK - skip
    low = _dot(_lower_rows(p, skip), p).astype(_BF16)
    zeros = jnp.zeros((skip, p.shape[1]), _BF16)
    return jnp.concatenate([piece for hd in range(GLA_HEADS)
                            for piece in (zeros, low[hd * rows:(hd + 1) * rows])], axis=0)


def _add_lower(s, upd, skip):
    if skip == 0:
        return s + upd
    rows = CHUNK - skip
    return jnp.concatenate([piece for hd in range(GLA_HEADS)
                            for piece in (s[hd * CHUNK:hd * CHUNK + skip],
                                          s[hd * CHUNK + skip:(hd + 1) * CHUNK] + upd[hd * rows:(hd + 1) * rows])],
                           axis=0)


def _layer_kernel(x_ref, mod_ref, w_all_ref, wg_ref, bg_ref, glanw_ref, gdnnw_ref, convw_ref, alog_ref, dtb_ref,
                  w_out_ref, lnw_ref, lnb_ref, out_ref, proj_ref, y_ref, sgla_ref, sgdn_ref, a_ref, qk_ref,
                  sol_ref, qd_ref, kd_ref, dl_ref, qkv_ref, hb_ref, *, tb, alpha):
    t = pl.program_id(1)
    n_chunks = tb // CHUNK

    @pl.when(t == 0)
    def _():
        proj_ref[0:SUBLANES, :] = jnp.zeros((SUBLANES, N_COLS), _F32)
        sgla_ref[...] = jnp.zeros_like(sgla_ref)
        sgdn_ref[...] = jnp.zeros_like(sgdn_ref)

    qkv_cols = slice(C_GDN_QKV, C_GDN_QKV + 3 * GDN_WIDTH)

    @pl.when(t > 0)
    def _():
        proj_ref[0:SUBLANES, qkv_cols] = proj_ref[tb:tb + SUBLANES, qkv_cols]

    shift = mod_ref[0:1, :]
    scale = mod_ref[1:2, :]
    gate = mod_ref[2:3, :]
    conv_w = convw_ref[...]
    group_rows = SOLVE_GROUP * CHUNK

    def project(gi):
        rs = slice(gi * group_rows, (gi + 1) * group_rows)
        hb_ref[...] = (x_ref[rs, :] * (1.0 + scale) + shift).astype(_BF16)
        for c0 in range(0, N_COLS, PROJ_PIECE):
            c1 = min(c0 + PROJ_PIECE, N_COLS)
            proj_ref[SUBLANES + gi * group_rows:SUBLANES + (gi + 1) * group_rows, c0:c1] = jnp.dot(
                hb_ref[...], w_all_ref[:, c0:c1], preferred_element_type=_F32)
            yield

    def conv_silu(c):
        base = SUBLANES + c * CHUNK
        conv = None
        for kk in range(CONV_WIDTH):
            off = base - (CONV_WIDTH - 1) + kk
            term = conv_w[kk:kk + 1, :] * proj_ref[off:off + CHUNK, qkv_cols]
            conv = term if conv is None else conv + term
        qkv_ref[c * CHUNK:(c + 1) * CHUNK, :] = conv * _sigmoid(conv)

    row = lax.broadcasted_iota(jnp.int32, (STACK, STACK), 0)
    col = lax.broadcasted_iota(jnp.int32, (STACK, STACK), 1)
    same_head = (row // CHUNK) == (col // CHUNK)
    causal = same_head & (row >= col)
    strict = same_head & (row > col)
    row_c = lax.broadcasted_iota(jnp.int32, (CHUNK, CHUNK), 0)
    col_c = lax.broadcasted_iota(jnp.int32, (CHUNK, CHUNK), 1)
    tri_lower = (row_c >= col_c).astype(_BF16)
    tri_upper4 = jnp.concatenate([(row_c <= col_c).astype(_BF16)] * GDN_HEADS, axis=1)
    lane_head = lax.broadcasted_iota(jnp.int32, (1, STACK), 1) // CHUNK
    neg_exp_alog = -jnp.exp(alog_ref[...])
    dt_bias = dtb_ref[...]
    bg = bg_ref[...]

    def gated_out(o_s, og_s, nw):
        o_n = o_s * lax.rsqrt(jnp.mean(o_s * o_s, axis=-1, keepdims=True) + RMS_EPS) * nw
        return (o_n * (og_s * _sigmoid(og_s))).astype(_BF16)

    def prepare(c):
        r0 = c * CHUNK
        rows = pl.ds(r0 + SUBLANES, CHUNK)

        small = proj_ref[rows, C_SMALL:C_SMALL + LANES]
        z = _dot(small, wg_ref[...]) + bg
        g_gla = (jnp.minimum(z, 0.0) - jnp.log1p(jnp.exp(-jnp.abs(z)))) * (1.0 / GLA_GATE_NORM)
        g_gdn = neg_exp_alog * _softplus(small + dt_bias)
        beta_all = _sigmoid(small)

        g_all = jnp.concatenate([g_gla, g_gdn], axis=1)
        g_hi, g_lo = _split_hi_lo(g_all)
        cum = (jnp.dot(tri_lower, g_hi, preferred_element_type=_F32)
               + jnp.dot(tri_lower, g_lo, preferred_element_type=_F32))
        b = cum[:, :GLA_QK]
        d_all = cum[:, GLA_QK:]
        gd_hi, gd_lo = _split_hi_lo(g_gdn)
        d_rows = _dot_tn(gd_hi, tri_upper4) + _dot_tn(gd_lo, tri_upper4)

        q = proj_ref[rows, C_GLA_Q:C_GLA_Q + GLA_QK] * (GLA_DK ** -0.5)
        k = proj_ref[rows, C_GLA_K:C_GLA_K + GLA_QK]
        v_s = _stack_heads(proj_ref[rows, C_GLA_V:C_GLA_V + GLA_WIDTH], GLA_DV)
        b_mid = b[CHUNK // 2 - 1:CHUNK // 2, :]
        b_last = b[CHUNK - 1:CHUNK, :]
        q_intra = jnp.where(same_head, _tile_heads(q * jnp.exp(b - b_mid)), 0.0)
        k_intra = _tile_heads((k * jnp.exp(b_mid - b)).astype(_BF16))
        q_dec = jnp.where(same_head, _tile_heads(q * jnp.exp(b)), 0.0)
        k_dec = jnp.where(same_head, _tile_heads(k * jnp.exp(b_last - b)), 0.0)
        state_t = sgla_ref[...]
        att = jnp.where(causal, _dot_nt(q_intra, k_intra), 0.0)
        o_s = _dot(att, v_s) + _dot_nt(q_dec, state_t)
        sgla_ref[...] = state_t * jnp.exp(b_last) + _dot_tn(v_s, k_dec)
        og_s = _stack_heads(proj_ref[rows, C_GLA_OG:C_GLA_OG + GLA_WIDTH], GLA_DV)
        y_s = gated_out(o_s, og_s, glanw_ref[...])
        for hd in range(GLA_HEADS):
            y_ref[pl.ds(r0, CHUNK), hd * LANES:(hd + 1) * LANES] = y_s[hd * CHUNK:(hd + 1) * CHUNK]

        qkv = qkv_ref[pl.ds(r0, CHUNK), :]
        q_s = _stack_heads(qkv[:, 0:GDN_WIDTH], GDN_DK)
        k_s = _stack_heads(qkv[:, GDN_WIDTH:2 * GDN_WIDTH], GDN_DK)
        v_s = _stack_heads(qkv[:, 2 * GDN_WIDTH:3 * GDN_WIDTH], GDN_DV)
        q_s = q_s * lax.rsqrt(jnp.sum(q_s * q_s, axis=-1, keepdims=True) + RMS_EPS) * (GDN_DK ** -0.5)
        k_s = k_s * lax.rsqrt(jnp.sum(k_s * k_s, axis=-1, keepdims=True) + RMS_EPS)
        beta_s = _stack_cols(beta_all, L_B)
        d_col = _stack_cols(d_all, L_A)
        e_d = _stack_cols(jnp.exp(d_all), L_A)
        e_rest = _stack_cols(jnp.exp(d_all[CHUNK - 1:CHUNK, :] - d_all), L_A)
        d_row = d_rows[L_A:L_A + 1, :]
        for hd in range(1, GDN_HEADS):
            d_row = jnp.where(lane_head == hd, d_rows[L_A + hd:L_A + hd + 1, :], d_row)
        decay = jnp.exp(jnp.where(causal, d_col - d_row, -jnp.inf))
        k_beta = k_s * beta_s
        aq = _dot_nt(jnp.concatenate([k_beta, q_s], axis=0), k_s)
        a_ref[c] = jnp.where(strict, aq[:STACK] * decay, 0.0).astype(_BF16)
        qk_ref[c] = (aq[STACK:] * decay).astype(_BF16)
        sol_ref[c] = jnp.concatenate([v_s * beta_s, k_beta * e_d], axis=1)
        qd_ref[c] = (q_s * e_d).astype(_BF16)
        kd_ref[c] = (k_s * e_rest).astype(_BF16)
        dl_ref[c] = d_all[CHUNK - SUBLANES:CHUNK, :]

    def solve(gi):
        base = gi * SOLVE_GROUP
        pows = [a_ref[base + j] for j in range(SOLVE_GROUP)]
        sols = [sol_ref[base + j] for j in range(SOLVE_GROUP)]
        sols = [s - _dot(p, s) for p, s in zip(pows, sols)]
        yield
        for lvl in range(5):
            skip = 2 ** (lvl + 1) if 2 ** (lvl + 1) >= SOLVE_SKIP_MIN else 0
            pows = [_square_lower(p, skip) for p in pows]
            sols = [_add_lower(s, _dot(_lower_rows(p, skip), s), skip) for p, s in zip(pows, sols)]
            if lvl < 4:
                yield
        for j in range(SOLVE_GROUP):
            sol_ref[base + j] = sols[j]
        yield


    def recur(c):
        r0 = c * CHUNK
        rows = pl.ds(r0 + SUBLANES, CHUNK)
        sol = sol_ref[c]
        u_s = sol[:, :GDN_DV]
        w_b = sol[:, GDN_DV:].astype(_BF16)
        q_d = qd_ref[c]
        k_d = kd_ref[c]
        d_last = dl_ref[c][SUBLANES - 1:SUBLANES, :]
        states = [sgdn_ref[hd] for hd in range(GDN_HEADS)]
        ws, qs = [], []
        for hd in range(GDN_HEADS):
            hs = slice(hd * CHUNK, (hd + 1) * CHUNK)
            wq = _dot(jnp.concatenate([w_b[hs], q_d[hs]], axis=0), states[hd])
            ws.append(wq[:CHUNK])
            qs.append(wq[CHUNK:])
        v_new = (u_s - jnp.concatenate(ws, axis=0)).astype(_BF16)
        o_s = jnp.concatenate(qs, axis=0) + _dot(qk_ref[c], v_new)
        for hd in range(GDN_HEADS):
            hs = slice(hd * CHUNK, (hd + 1) * CHUNK)
            dec = jnp.exp(d_last[:, L_A + hd:L_A + hd + 1])
            sgdn_ref[hd] = dec * states[hd] + _dot_tn(k_d[hs], v_new[hs])
        og_s = _stack_heads(proj_ref[rows, C_GDN_OG:C_GDN_OG + GDN_WIDTH], GDN_DV)
        y_s = gated_out(o_s, og_s, gdnnw_ref[...])
        for hd in range(GDN_HEADS):
            y_ref[pl.ds(r0, CHUNK), (GLA_HEADS + hd) * LANES:(GLA_HEADS + hd + 1) * LANES] = \
                y_s[hd * CHUNK:(hd + 1) * CHUNK]

    def finish(gi):
        for r0 in range(gi * group_rows, (gi + 1) * group_rows, FINISH_ROWS):
            rs = slice(r0, r0 + FINISH_ROWS)
            y = jnp.dot(y_ref[rs, :], w_out_ref[...], preferred_element_type=_F32)
            r = alpha * x_ref[rs, :] + (1.0 + gate) * y
            mu = jnp.mean(r, axis=-1, keepdims=True)
            rc = r - mu
            var = jnp.mean(rc * rc, axis=-1, keepdims=True)
            out_ref[rs, :] = rc * lax.rsqrt(var + LN_EPS) * lnw_ref[...] + lnb_ref[...]
            yield

    def prepare_group(gi):
        for j in range(SOLVE_GROUP):
            conv_silu(gi * SOLVE_GROUP + j)
            prepare(gi * SOLVE_GROUP + j)
            yield

    def recur_group(gi):
        for j in range(SOLVE_GROUP):
            recur(gi * SOLVE_GROUP + j)
            yield

    def emit(*stages):
        live = list(stages)
        while live:
            for st in list(live):
                if next(st, StopIteration) is StopIteration:
                    live.remove(st)

    n_groups = n_chunks // SOLVE_GROUP
    for step in range(n_groups + 4):
        stages = []
        if step < n_groups:
            stages.append(project(step))
        if 1 <= step <= n_groups:
            stages.append(prepare_group(step - 1))
        if 2 <= step <= n_groups + 1:
            stages.append(solve(step - 2))
        if 3 <= step <= n_groups + 2:
            stages.append(recur_group(step - 3))
        if step >= 4:
            stages.append(finish(step - 4))
        emit(*stages)


def _modulation(c, w_ada, b_ada):
    bsz, d = c.shape
    n = w_ada.shape[1]
    return pl.pallas_call(
        _mod_kernel,
        grid=(n // d,),
        in_specs=[pl.BlockSpec((bsz, d), lambda j: (0, 0)),
                  pl.BlockSpec((d, d), lambda j: (0, j)),
                  pl.BlockSpec((1, d), lambda j: (0, j))],
        out_specs=pl.BlockSpec((bsz, d), lambda j: (0, j)),
        out_shape=jax.ShapeDtypeStruct((bsz, n), _F32),
        name="adaln_modulation",
    )(c, w_ada, b_ada.reshape(1, n))


def _pack_weights(w_in, gla_w_gate_up, gdn_a_log, gdn_dt_bias):
    o = 0
    seg = {}
    for name, size in (("gla_q", GLA_QK), ("gla_k", GLA_QK), ("gla_v", GLA_WIDTH), ("gla_lr", GLA_GATE_RANK),
                       ("gla_og", GLA_WIDTH), ("gdn_qkv", 3 * GDN_WIDTH), ("gdn_a", GDN_HEADS),
                       ("gdn_b", GDN_HEADS), ("gdn_og", GDN_WIDTH)):
        seg[name] = w_in[:, o:o + size]
        o += size
    pad = jnp.zeros((w_in.shape[0], LANES - L_B - GDN_HEADS), w_in.dtype)
    w_all = jnp.concatenate([seg["gla_q"], seg["gla_k"], seg["gla_v"], seg["gla_og"], seg["gdn_qkv"],
                             seg["gdn_og"], seg["gla_lr"], seg["gdn_a"], seg["gdn_b"], pad], axis=1)
    wg = jnp.zeros((LANES, GLA_QK), _F32).at[L_LR:L_LR + GLA_GATE_RANK, :].set(gla_w_gate_up)
    alog_row = jnp.zeros((1, LANES), _F32).at[0, L_A:L_A + GDN_HEADS].set(gdn_a_log)
    dtb_row = jnp.zeros((1, LANES), _F32).at[0, L_A:L_A + GDN_HEADS].set(gdn_dt_bias)
    return w_all.astype(_BF16), wg.astype(_BF16), alog_row, dtb_row


def _hybrid_layer(x, c, w_ada, b_ada, w_in, gla_w_gate_up, gla_b_gate, gla_norm_w, gdn_conv_w, gdn_a_log,
                  gdn_dt_bias, gdn_norm_w, w_out, ln_w, ln_b, *, alpha):
    bsz, seq, d = x.shape
    tb = min(TIME_BLOCK, seq)
    assert d == D_MODEL and seq % tb == 0 and tb % CHUNK == 0
    mod = _modulation(c, w_ada, b_ada).reshape(bsz, 3, d)
    w_all, wg, alog_row, dtb_row = _pack_weights(w_in, gla_w_gate_up, gdn_a_log, gdn_dt_bias)
    n_chunks = tb // CHUNK
    assert GLA_HEADS == GDN_HEADS and n_chunks % SOLVE_GROUP == 0
    const = lambda b, t: (0, 0)
    return pl.pallas_call(
        functools.partial(_layer_kernel, tb=tb, alpha=alpha),
        grid=(bsz, seq // tb),
        in_specs=[
            pl.BlockSpec((None, tb, d), lambda b, t: (b, t, 0)),
            pl.BlockSpec((None, 3, d), lambda b, t: (b, 0, 0)),
            pl.BlockSpec((d, N_COLS), const),
            pl.BlockSpec((LANES, GLA_QK), const),
            pl.BlockSpec((1, GLA_QK), const),
            pl.BlockSpec((1, GLA_DV), const),
            pl.BlockSpec((1, GDN_DV), const),
            pl.BlockSpec((CONV_WIDTH, 3 * GDN_WIDTH), const),
            pl.BlockSpec((1, LANES), const),
            pl.BlockSpec((1, LANES), const),
            pl.BlockSpec((d, d), const),
            pl.BlockSpec((1, d), const),
            pl.BlockSpec((1, d), const),
        ],
        out_specs=pl.BlockSpec((None, tb, d), lambda b, t: (b, t, 0)),
        out_shape=jax.ShapeDtypeStruct((bsz, seq, d), _F32),
        scratch_shapes=[
            pltpu.VMEM((tb + SUBLANES, N_COLS), _F32),
            pltpu.VMEM((tb, d), _BF16),
            pltpu.VMEM((GLA_DV, GLA_QK), _F32),
            pltpu.VMEM((GDN_HEADS, GDN_DK, GDN_DV), _F32),
            pltpu.VMEM((n_chunks, STACK, STACK), _BF16),
            pltpu.VMEM((n_chunks, STACK, STACK), _BF16),
            pltpu.VMEM((n_chunks, STACK, GDN_DV + GDN_DK), _F32),
            pltpu.VMEM((n_chunks, STACK, GDN_DK), _BF16),
            pltpu.VMEM((n_chunks, STACK, GDN_DK), _BF16),
            pltpu.VMEM((n_chunks, SUBLANES, LANES), _F32),
            pltpu.VMEM((tb, 3 * GDN_WIDTH), _F32),
            pltpu.VMEM((SOLVE_GROUP * CHUNK, d), _BF16),
        ],
        compiler_params=pltpu.CompilerParams(dimension_semantics=("arbitrary", "arbitrary"),
                                             vmem_limit_bytes=VMEM_LIMIT_BYTES),
        name="hybrid_layer",
    )(x, mod, w_all, wg, gla_b_gate.reshape(1, -1), gla_norm_w.reshape(1, -1), gdn_norm_w.reshape(1, -1),
      gdn_conv_w, alog_row, dtb_row,
      w_out.astype(_BF16), ln_w.reshape(1, -1), ln_b.reshape(1, -1))


def kernel(x, c, w_ada, b_ada, w_in, gla_w_gate_up, gla_b_gate, gla_norm_w, gdn_conv_w, gdn_a_log, gdn_dt_bias, gdn_norm_w, w_out, ln_w, ln_b):
    depth = w_ada.shape[0]
    alpha = (2.0 * depth) ** 0.25
    for layer in range(depth):
        x = _hybrid_layer(x, c, w_ada[layer], b_ada[layer], w_in[layer], gla_w_gate_up[layer],
                          gla_b_gate[layer], gla_norm_w[layer], gdn_conv_w[layer], gdn_a_log[layer],
                          gdn_dt_bias[layer], gdn_norm_w[layer], w_out[layer], ln_w[layer], ln_b[layer],
                          alpha=alpha)
    return x
```
